```python
import math
import jax, jax.numpy as jnp
from jax import lax
import numpy as np

D_MODEL = 1024
BATCH = 8
SEQ = 2048
DEPTH = 4
DEC_BATCH = 2
DEC_SEQ = 8192
PAST_LEN = 128

N_MEM = 256
EPS = 1e-6
NEG_INF = -1e30

ATT_GROUPS = ((128, 1), (512, 4), (2048, 16))
ATT_HEADS_PER_GROUP = 4
ATT_HEADS = ATT_HEADS_PER_GROUP * len(ATT_GROUPS)
ATT_HEAD_DIM = 64
ATT_WIDTH = ATT_HEADS * ATT_HEAD_DIM

MLSTM_HEADS = 4
MLSTM_HEAD_DIM = 192
MLSTM_WIDTH = MLSTM_HEADS * MLSTM_HEAD_DIM
MLSTM_CHUNK = 64

XATT_HEADS = 4
XATT_HEAD_DIM = 192
XATT_WIDTH = XATT_HEADS * XATT_HEAD_DIM

N_BRANCH = 3
BRANCH_WIDTH = 768
CONV_W = 3
FFN_DIM = 2816

IN_SIZES = (ATT_WIDTH, ATT_WIDTH, ATT_WIDTH,
            MLSTM_WIDTH, MLSTM_WIDTH, MLSTM_WIDTH, MLSTM_WIDTH, 4 * MLSTM_HEADS,
            XATT_WIDTH, N_BRANCH * D_MODEL)
IN_DIM = sum(IN_SIZES)

kernel_name = "hybrid_dilated_mlstm_memory_encoder"


def rmsnorm(x, g):
    xf = x.astype(jnp.float32)
    y = xf * lax.rsqrt(jnp.mean(xf * xf, axis=-1, keepdims=True) + EPS)
    return (y * g.astype(jnp.float32)).astype(x.dtype)


def dwconv(x, w, b):
    C = x.shape[-1]
    y = lax.conv_general_dilated(
        x, w[:, None, :].astype(x.dtype), window_strides=(1,),
        padding=((CONV_W // 2, CONV_W // 2),),
        dimension_numbers=("NWC", "WIO", "NWC"), feature_group_count=C)
    return y + b.astype(x.dtype)


def alibi_slopes(n):
    return jnp.exp2(-8.0 * jnp.arange(1, n + 1, dtype=jnp.float32) / n)


def dilated_group_attention(q, k, v, dil, radius, slopes):
    B, S, H, Dh = q.shape
    R = radius
    Lc = S // dil
    nb = -(-Lc // R)
    Lp = nb * R

    def to_classes(t):
        t = t.reshape(B, Lc, dil, H, Dh).transpose(0, 2, 1, 3, 4)
        return jnp.pad(t, ((0, 0), (0, 0), (0, Lp - Lc), (0, 0), (0, 0)))

    def neighbours(t):
        tp = jnp.pad(t, ((0, 0), (0, 0), (R, R), (0, 0), (0, 0))).reshape(B, dil, nb + 2, R, H, Dh)
        return jnp.concatenate([tp[:, :, :-2], tp[:, :, 1:-1], tp[:, :, 2:]], axis=3)

    qb = to_classes(q).reshape(B, dil, nb, R, H, Dh)
    kb = neighbours(to_classes(k))
    vb = neighbours(to_classes(v))

    s = jnp.einsum("bcnqhd,bcnkhd->bcnhqk", qb, kb,
                   preferred_element_type=jnp.float32) * (Dh ** -0.5)
    iq = jnp.arange(R)
    jk = jnp.arange(3 * R)
    blk = jnp.arange(nb)
    delta = jk[None, :] - R - iq[:, None]
    uk = (blk[:, None] - 1) * R + jk[None, :]
    valid = (jnp.abs(delta) <= R)[None] & ((uk >= 0) & (uk < Lc))[:, None, :]
    bias = -slopes[:, None, None] * (dil * jnp.abs(delta)).astype(jnp.float32)[None]
    s = jnp.where(valid[None, None, :, None], s + bias[None, None, None], NEG_INF)
    lse = jax.nn.logsumexp(s, axis=-1)
    p = jnp.exp(s - lse[..., None]).astype(v.dtype)
    o = jnp.einsum("bcnhqk,bcnkhd->bcnqhd", p, vb)
    o = o.reshape(B, dil, Lp, H, Dh)[:, :, :Lc].transpose(0, 2, 1, 3, 4).reshape(B, S, H, Dh)
    lse = lse.transpose(0, 1, 2, 4, 3).reshape(B, dil, Lp, H)[:, :, :Lc]
    lse = lse.transpose(0, 2, 1, 3).reshape(B, S, H)
    return o, lse


def mlstm_chunkwise(q, k, v, log_i, log_f):
    B, S, H, Dh = q.shape
    L = MLSTM_CHUNK
    nc = S // L
    f32 = jnp.float32

    def chunks(t):
        t = t.astype(f32)
        return t.reshape((B, nc, L) + t.shape[2:]).swapaxes(0, 1)

    causal = jnp.tril(jnp.ones((L, L), dtype=bool))

    def step(carry, inp):
        C, n, m = carry
        qj, kj, vj, li, lf = inp
        b = jnp.cumsum(lf, axis=1).swapaxes(1, 2)
        li = li.swapaxes(1, 2)
        dmat = jnp.where(causal, b[..., :, None] - b[..., None, :] + li[..., None, :], -jnp.inf)
        inter = b + m[..., None]
        m_row = jnp.maximum(inter, jnp.max(dmat, axis=-1))
        w = jnp.exp(dmat - m_row[..., None])
        a = jnp.exp(inter - m_row)
        sqk = jnp.einsum("blhd,bshd->bhls", qj, kj) * w
        num = jnp.einsum("bhls,bshd->bhld", sqk, vj) + a[..., None] * jnp.einsum("blhd,bhde->bhle", qj, C)
        den = jnp.sum(sqk, axis=-1) + a * jnp.einsum("blhd,bhd->bhl", qj, n)
        hj = num / jnp.maximum(jnp.abs(den), jnp.exp(-m_row))[..., None]
        b_last = b[..., -1]
        g = b_last[..., None] - b + li
        m_new = jnp.maximum(b_last + m, jnp.max(g, axis=-1))
        kw = kj * jnp.exp(g - m_new[..., None]).swapaxes(1, 2)[..., None]
        decay = jnp.exp(b_last + m - m_new)
        C = decay[..., None, None] * C + jnp.einsum("bshd,bshe->bhde", kw, vj)
        n = decay[..., None] * n + jnp.sum(kw, axis=1)
        return (C, n, m_new), hj.swapaxes(1, 2)

    init = (jnp.zeros((B, H, Dh, Dh), f32), jnp.zeros((B, H, Dh), f32), jnp.zeros((B, H), f32))
    xs = (chunks(q) * (Dh ** -0.5), chunks(k), chunks(v), chunks(log_i), chunks(log_f))
    _, hs = lax.scan(step, init, xs)
    return hs.swapaxes(0, 1).reshape(B, S, H, Dh)


def token_mixer(h, mem_n, w_in, mlstm_conv_w, mlstm_conv_b, mlstm_gate_b, att_q_g, att_k_g,
                xatt_q_g, xatt_k_g, w_mem_kv, mlstm_h_g, w_branch, w_out):
    B, S, _ = h.shape
    points = [int(p) for p in np.cumsum(IN_SIZES)[:-1]]
    aq, ak, av, mq, mk, mv, mo, mif, xq, gpre = jnp.split(h @ w_in, points, axis=-1)

    aq = rmsnorm(aq.reshape(B, S, ATT_HEADS, ATT_HEAD_DIM), att_q_g)
    ak = rmsnorm(ak.reshape(B, S, ATT_HEADS, ATT_HEAD_DIM), att_k_g)
    av = av.reshape(B, S, ATT_HEADS, ATT_HEAD_DIM)
    slopes = alibi_slopes(ATT_HEADS)
    outs, lses = [], []
    for gi, (win, dil) in enumerate(ATT_GROUPS):
        sl = slice(gi * ATT_HEADS_PER_GROUP, (gi + 1) * ATT_HEADS_PER_GROUP)
        o, lse = dilated_group_attention(aq[:, :, sl], ak[:, :, sl], av[:, :, sl],
                                         dil, win // (2 * dil), slopes[sl])
        outs.append(o)
        lses.append(lse)
    alpha = jax.nn.softmax(jnp.stack(lses, axis=0), axis=0)
    att = jnp.concatenate([o * alpha[gi][..., None].astype(o.dtype) for gi, o in enumerate(outs)],
                          axis=2).reshape(B, S, ATT_WIDTH)

    qk = jax.nn.silu(dwconv(jnp.concatenate([mq, mk], axis=-1), mlstm_conv_w, mlstm_conv_b))
    mq, mk = jnp.split(qk, 2, axis=-1)
    gates = (mif + mlstm_gate_b).astype(jnp.float32).reshape(B, S, 4, MLSTM_HEADS)
    li_f, lf_f = gates[:, :, 0], jax.nn.log_sigmoid(gates[:, :, 1])
    li_b, lf_b = gates[:, :, 2], jax.nn.log_sigmoid(gates[:, :, 3])
    q4 = mq.reshape(B, S, MLSTM_HEADS, MLSTM_HEAD_DIM)
    k4 = mk.reshape(B, S, MLSTM_HEADS, MLSTM_HEAD_DIM)
    v4 = mv.reshape(B, S, MLSTM_HEADS, MLSTM_HEAD_DIM)
    flip = lambda t: jnp.flip(t, axis=1)
    h_fwd = mlstm_chunkwise(q4, k4, v4, li_f, lf_f)
    h_bwd = flip(mlstm_chunkwise(flip(q4), flip(k4), flip(v4), flip(li_b), flip(lf_b)))
    hm = rmsnorm(h_fwd + h_bwd, mlstm_h_g.reshape(MLSTM_HEADS, MLSTM_HEAD_DIM))
    hm = hm.reshape(B, S, MLSTM_WIDTH).astype(h.dtype) * jax.nn.sigmoid(mo)

    M = mem_n.shape[1]
    mkv = mem_n @ w_mem_kv
    xk, xv = jnp.split(mkv, 2, axis=-1)
    xq = rmsnorm(xq.reshape(B, S, XATT_HEADS, XATT_HEAD_DIM), xatt_q_g)
    xk = rmsnorm(xk.reshape(B, M, XATT_HEADS, XATT_HEAD_DIM), xatt_k_g)
    xv = xv.reshape(B, M, XATT_HEADS, XATT_HEAD_DIM)
    s = jnp.einsum("bshd,bmhd->bhsm", xq, xk, preferred_element_type=jnp.float32) * (XATT_HEAD_DIM ** -0.5)
    p = jax.nn.softmax(s, axis=-1).astype(xv.dtype)
    xo = jnp.einsum("bhsm,bmhd->bshd", p, xv).reshape(B, S, XATT_WIDTH)

    gate = jax.nn.sigmoid(gpre.reshape(B, S, N_BRANCH, D_MODEL))
    br = jnp.stack([att.astype(h.dtype), hm, xo], axis=2)
    proj = jnp.einsum("bsnc,ncd->bsnd", br, w_branch)
    merged = jnp.sum(gate * proj, axis=2)
    return merged @ w_out


def conv_ffn(h, w_up, ffn_conv_w, ffn_conv_b, w_down):
    u = dwconv(h @ w_up, ffn_conv_w, ffn_conv_b)
    a, val = jnp.split(u, 2, axis=-1)
    return (jax.nn.gelu(a) * val) @ w_down


def setup_inputs(seed: int = 0) -> dict:
    key = jax.random.key(seed)
    ks = jax.random.split(key, 23)
    f32 = jnp.float32

    def nrm(k, shape, scale):
        return jax.random.normal(k, shape, f32) * scale

    def gain(k, shape):
        return 1.0 + 0.02 * jax.random.normal(k, shape, f32)

    fbias = jnp.linspace(3.0, 6.0, MLSTM_HEADS, dtype=f32)
    gate_base = jnp.concatenate([jnp.zeros((MLSTM_HEADS,), f32), fbias,
                                 jnp.zeros((MLSTM_HEADS,), f32), fbias])
    return {
        "x_prompt": nrm(ks[0], (BATCH, SEQ, D_MODEL), 1.0),
        "x_sample": nrm(ks[1], (DEC_BATCH, DEC_SEQ, D_MODEL), 1.0),
        "mem_prompt": nrm(ks[2], (BATCH, N_MEM, D_MODEL), 1.0),
        "mem_sample": nrm(ks[3], (DEC_BATCH, N_MEM, D_MODEL), 1.0),
        "norm_mix_g": gain(ks[4], (DEPTH, D_MODEL)),
        "norm_mem_g": gain(ks[5], (DEPTH, D_MODEL)),
        "w_in": nrm(ks[6], (DEPTH, D_MODEL, IN_DIM), D_MODEL ** -0.5),
        "mlstm_conv_w": nrm(ks[7], (DEPTH, CONV_W, 2 * MLSTM_WIDTH), CONV_W ** -0.5),
        "mlstm_conv_b": nrm(ks[8], (DEPTH, 2 * MLSTM_WIDTH), 0.02),
        "mlstm_gate_b": gate_base[None] + nrm(ks[9], (DEPTH, 4 * MLSTM_HEADS), 0.1),
        "att_q_g": gain(ks[10], (DEPTH, ATT_HEAD_DIM)),
        "att_k_g": gain(ks[11], (DEPTH, ATT_HEAD_DIM)),
        "xatt_q_g": gain(ks[12], (DEPTH, XATT_HEAD_DIM)),
        "xatt_k_g": gain(ks[13], (DEPTH, XATT_HEAD_DIM)),
        "w_mem_kv": nrm(ks[14], (DEPTH, D_MODEL, 2 * XATT_WIDTH), D_MODEL ** -0.5),
        "mlstm_h_g": gain(ks[15], (DEPTH, MLSTM_WIDTH)),
        "w_branch": nrm(ks[16], (DEPTH, N_BRANCH, BRANCH_WIDTH, D_MODEL), BRANCH_WIDTH ** -0.5),
        "w_out": nrm(ks[17], (DEPTH, D_MODEL, D_MODEL), D_MODEL ** -0.5),
        "norm_ffn_g": gain(ks[18], (DEPTH, D_MODEL)),
        "w_up": nrm(ks[19], (DEPTH, D_MODEL, 2 * FFN_DIM), D_MODEL ** -0.5),
        "ffn_conv_w": nrm(ks[20], (DEPTH, CONV_W, 2 * FFN_DIM), CONV_W ** -0.5),
        "ffn_conv_b": nrm(ks[21], (DEPTH, 2 * FFN_DIM), 0.02),
        "w_down": nrm(ks[22], (DEPTH, FFN_DIM, D_MODEL), FFN_DIM ** -0.5),
    }


def reference(x_prompt, x_sample, mem_prompt, mem_sample, norm_mix_g, norm_mem_g, w_in,
              mlstm_conv_w, mlstm_conv_b, mlstm_gate_b, att_q_g, att_k_g, xatt_q_g, xatt_k_g,
              w_mem_kv, mlstm_h_g, w_branch, w_out, norm_ffn_g, w_up, ffn_conv_w, ffn_conv_b,
              w_down):
    def trunk(x, mem):
        for l in range(DEPTH):
            h = rmsnorm(x, norm_mix_g[l])
            mem_n = rmsnorm(mem, norm_mem_g[l])
            x = x + token_mixer(h, mem_n, w_in[l], mlstm_conv_w[l], mlstm_conv_b[l], mlstm_gate_b[l],
                                att_q_g[l], att_k_g[l], xatt_q_g[l], xatt_k_g[l], w_mem_kv[l],
                                mlstm_h_g[l], w_branch[l], w_out[l]).astype(x.dtype)
            h = rmsnorm(x, norm_ffn_g[l])
            x = x + conv_ffn(h, w_up[l], ffn_conv_w[l], ffn_conv_b[l], w_down[l]).astype(x.dtype)
        return x

    y_prompt = trunk(x_prompt, mem_prompt)
    y_sample = trunk(x_sample, mem_sample)
    return (y_prompt, y_sample)
```

```python
import functools
import math

import jax
import jax.numpy as jnp
from jax import lax
from jax.experimental import pallas as pl
from jax.experimental.pallas import tpu as pltpu

F32 = jnp.float32
BF16 = jnp.bfloat16

D_MODEL = 1024
DEPTH = 4
EPS = 1e-6
NEG_INF = -1e30

ATT_GROUPS = ((128, 1), (512, 4), (2048, 16))
ATT_HEADS_PER_GROUP = 4
ATT_HEAD_DIM = 64
ATT_HEADS = ATT_HEADS_PER_GROUP * len(ATT_GROUPS)
ATT_WIDTH = ATT_HEADS * ATT_HEAD_DIM
ATT_GROUP_WIDTH = ATT_HEADS_PER_GROUP * ATT_HEAD_DIM
ATT_RADIUS = 64
ATT_TQ = 128

HEADS = 4
HEAD_DIM = 192
HEAD_PAD = 256
WIDTH = HEADS * HEAD_DIM
WIDTH_PAD = HEADS * HEAD_PAD
MLSTM_CHUNK = 256
N_MEM_ROWS = 256
N_GATES = 4 * HEADS
GATE_PAD = 128
HALO_ROWS = 16

FFN_DIM = 2816
FFN_CHUNK = 1408

PROJ_TILE = 1024
MERGE_TILE = 512
FFN_TILE = 512
VMEM_LIMIT_BYTES = 56 * 1024 * 1024


def _cparams(sem):
    return pltpu.CompilerParams(dimension_semantics=sem, vmem_limit_bytes=VMEM_LIMIT_BYTES)


def _rmsnorm_rows(x, g):
    ms = jnp.mean(x * x, axis=-1, keepdims=True)
    return x * lax.rsqrt(ms + EPS) * g


def _sigmoid(y):
    return 1.0 / (1.0 + jnp.exp(-y))


def _log_sigmoid(y):
    return -(jnp.maximum(-y, 0.0) + jnp.log(1.0 + jnp.exp(-jnp.abs(y))))


def _proj_kernel(*refs, modes, with_gates):
    if with_gates:
        x_ref, g_ref, w_ref, gain_ref, bd_ref, wg_ref, bg_ref, o_ref, gates_ref, h_scr = refs
    else:
        x_ref, g_ref, w_ref, gain_ref, bd_ref, o_ref, h_scr = refs
    j = pl.program_id(1)

    @pl.when(j == 0)
    def _():
        h = _rmsnorm_rows(x_ref[...], g_ref[...]).astype(BF16)
        h_scr[...] = h
        if with_gates:
            gates_ref[...] = jnp.dot(h, wg_ref[...], preferred_element_type=F32) + bg_ref[...]

    y = jnp.dot(h_scr[...], w_ref[...], preferred_element_type=F32)
    wc = y.shape[1]

    for mode in sorted(set(modes)):
        pred = functools.reduce(jnp.logical_or, [j == k for k, m in enumerate(modes) if m == mode])

        @pl.when(pred)
        def _(mode=mode):
            if mode == "raw":
                r = y
            elif mode == "sig":
                r = _sigmoid(y)
            elif mode == "norm64":
                sq = (y * y).astype(BF16)
                ss = jnp.concatenate(
                    [jnp.dot(sq[:, c:c + HEAD_PAD], bd_ref[...], preferred_element_type=F32)
                     for c in range(0, wc, HEAD_PAD)], axis=1)
                r = y * lax.rsqrt(ss * (1.0 / ATT_HEAD_DIM) + EPS) * gain_ref[0]
            else:
                parts = []
                for c in range(0, wc, HEAD_PAD):
                    yh = y[:, c:c + HEAD_PAD]
                    ms = jnp.sum(yh * yh, axis=1, keepdims=True) * (1.0 / HEAD_DIM)
                    parts.append(yh * lax.rsqrt(ms + EPS))
                r = jnp.concatenate(parts, axis=1) * gain_ref[0]
            o_ref[0] = r.astype(o_ref.dtype)


def _proj(x, g, w, gains, modes, tile, wg=None, bg=None):
    n_tok, d = x.shape
    n_chunks = len(modes)
    wc = w.shape[1] // n_chunks
    with_gates = wg is not None
    bd = (jnp.arange(HEAD_PAD)[:, None] // ATT_HEAD_DIM
          == jnp.arange(HEAD_PAD)[None, :] // ATT_HEAD_DIM).astype(BF16)
    in_specs = [
        pl.BlockSpec((tile, d), lambda i, j: (i, 0)),
        pl.BlockSpec((1, d), lambda i, j: (0, 0)),
        pl.BlockSpec((d, wc), lambda i, j: (0, j)),
        pl.BlockSpec((1, 1, wc), lambda i, j: (j, 0, 0)),
        pl.BlockSpec((HEAD_PAD, HEAD_PAD), lambda i, j: (0, 0)),
    ]
    args = [x, g.reshape(1, d), w, gains, bd]
    out_shape = [jax.ShapeDtypeStruct((n_chunks, n_tok, wc), BF16)]
    out_specs = [pl.BlockSpec((1, tile, wc), lambda i, j: (j, i, 0))]
    if with_gates:
        in_specs += [pl.BlockSpec((d, GATE_PAD), lambda i, j: (0, 0)),
                     pl.BlockSpec((1, GATE_PAD), lambda i, j: (0, 0))]
        args += [wg, bg]
        out_shape.append(jax.ShapeDtypeStruct((n_tok, GATE_PAD), F32))
        out_specs.append(pl.BlockSpec((tile, GATE_PAD), lambda i, j: (i, 0)))
    res = pl.pallas_call(
        functools.partial(_proj_kernel, modes=tuple(modes), with_gates=with_gates),
        grid=(n_tok // tile, n_chunks),
        in_specs=in_specs,
        out_specs=out_specs,
        out_shape=out_shape,
        scratch_shapes=[pltpu.VMEM((tile, d), BF16)],
        compiler_params=_cparams(("parallel", "arbitrary")),
        name="proj",
    )(*args)
    return res if with_gates else res[0]


def _att_kernel(q_ref, kp_ref, kc_ref, kn_ref, vp_ref, vc_ref, vn_ref, o_ref, l_ref, *,
                dil, lc, slopes):
    tq, r = ATT_TQ, ATT_RADIUS
    u0 = pl.program_id(2) * tq
    q = q_ref[0]
    kwin = jnp.concatenate([kp_ref[0][tq - r:], kc_ref[0], kn_ref[0][:r]], axis=0)
    vwin = jnp.concatenate([vp_ref[0][tq - r:], vc_ref[0], vn_ref[0][:r]], axis=0)
    row = lax.broadcasted_iota(jnp.int32, (tq, tq + 2 * r), 0)
    col = lax.broadcasted_iota(jnp.int32, (tq, tq + 2 * r), 1)
    dist = jnp.abs(col - r - row)
    key_pos = u0 - r + col
    valid = (dist <= r) & (key_pos >= 0) & (key_pos < lc)
    dist_f = dist.astype(F32) * float(dil)
    lane = lax.broadcasted_iota(jnp.int32, (1, ATT_GROUP_WIDTH), 1)
    o_acc = jnp.zeros((tq, ATT_GROUP_WIDTH), F32)
    l_acc = jnp.zeros((tq, ATT_GROUP_WIDTH), F32)
    for h in range(ATT_HEADS_PER_GROUP):
        in_head = (lane >= h * ATT_HEAD_DIM) & (lane < (h + 1) * ATT_HEAD_DIM)
        qh = jnp.where(in_head, q, jnp.zeros_like(q))
        s = lax.dot_general(qh, kwin, (((1,), (1,)), ((), ())), preferred_element_type=F32)
        s = jnp.where(valid, s - slopes[h] * dist_f, NEG_INF)
        m = jnp.max(s, axis=1, keepdims=True)
        p = jnp.exp(s - m)
        l = jnp.sum(p, axis=1, keepdims=True)
        oh = jnp.dot(p.astype(BF16), vwin, preferred_element_type=F32)
        o_acc = jnp.where(in_head, oh * (1.0 / l), o_acc)
        l_acc = jnp.where(in_head, m + jnp.log(l), l_acc)
    o_ref[...] = o_acc.astype(o_ref.dtype)
    l_ref[...] = l_acc


def _attention_group(qkv, gi, batch, seq):
    _, dil = ATT_GROUPS[gi]
    n_tok = batch * seq
    lc = seq // dil
    nblk = lc // ATT_TQ
    ng = len(ATT_GROUPS)
    slopes = tuple(2.0 ** (-8.0 * (gi * ATT_HEADS_PER_GROUP + h + 1) / ATT_HEADS)
                   for h in range(ATT_HEADS_PER_GROUP))
    view = qkv.reshape(3, n_tok // dil, dil * ATT_WIDTH)

    def spec(which, shift):
        def imap(b, r, u):
            uu = jnp.clip(u + shift, 0, nblk - 1)
            return (which, b * nblk + uu, r * ng + gi)
        return pl.BlockSpec((1, ATT_TQ, ATT_GROUP_WIDTH), imap)

    out_spec = pl.BlockSpec((ATT_TQ, ATT_GROUP_WIDTH), lambda b, r, u: (b * nblk + u, r))
    o, lse = pl.pallas_call(
        functools.partial(_att_kernel, dil=dil, lc=lc, slopes=slopes),
        grid=(batch, dil, nblk),
        in_specs=[spec(0, 0), spec(1, -1), spec(1, 0), spec(1, 1), spec(2, -1), spec(2, 0), spec(2, 1)],
        out_specs=[out_spec, out_spec],
        out_shape=[jax.ShapeDtypeStruct((n_tok // dil, dil * ATT_GROUP_WIDTH), BF16),
                   jax.ShapeDtypeStruct((n_tok // dil, dil * ATT_GROUP_WIDTH), F32)],
        compiler_params=_cparams(("parallel", "parallel", "parallel")),
        name=f"att_g{gi}",
    )(view, view, view, view, view, view, view)
    return o.reshape(n_tok, ATT_GROUP_WIDTH), lse.reshape(n_tok, ATT_GROUP_WIDTH)


def _mlstm_kernel(q_ref, k_ref, v_ref, qp_ref, qn_ref, kp_ref, kn_ref, gc_ref, gr_ref,
                  cwq_ref, cbq_ref, cwk_ref, cbk_ref, o_ref, c_scr, m_scr, *, nc):
    L = MLSTM_CHUNK
    d = pl.program_id(1)
    c = pl.program_id(2)
    ce = jnp.where(d == 0, c, nc - 1 - c)

    @pl.when(c == 0)
    def _():
        c_scr[...] = jnp.zeros_like(c_scr)
        m_scr[...] = jnp.zeros_like(m_scr)

    rows = lax.broadcasted_iota(jnp.int32, (L, 1), 0)
    has_prev = (ce > 0).astype(F32)
    has_next = (ce < nc - 1).astype(F32)

    def conv_silu(z_ref, zp_ref, zn_ref, w_ref, b_ref):
        z = z_ref[0, 0].astype(F32)
        prev_row = zp_ref[0, 0][HALO_ROWS - 1:HALO_ROWS, :].astype(F32) * has_prev
        next_row = zn_ref[0, 0][0:1, :].astype(F32) * has_next
        zm = jnp.where(rows == 0, prev_row, pltpu.roll(z, 1, 0))
        zp = jnp.where(rows == L - 1, next_row, pltpu.roll(z, L - 1, 0))
        w = w_ref[...]
        u = w[0:1] * zm + w[1:2] * z + w[2:3] * zp + b_ref[...]
        return u * _sigmoid(u)

    q = conv_silu(q_ref, qp_ref, qn_ref, cwq_ref, cbq_ref) * (HEAD_DIM ** -0.5)
    k = conv_silu(k_ref, kp_ref, kn_ref, cwk_ref, cbk_ref)
    v = v_ref[0, 0]

    gc = gc_ref[0, 0]
    gr = gr_ref[0, 0]
    lfc = _log_sigmoid(gc)
    lfr = _log_sigmoid(gr)

    sign = 1 - 2 * d
    ti = lax.broadcasted_iota(jnp.int32, (L, L), 0)
    si = lax.broadcasted_iota(jnp.int32, (L, L), 1)
    tri = (ti - si) * sign >= 0
    tri_t = (si - ti) * sign >= 0
    lane = lax.broadcasted_iota(jnp.int32, (1, HEAD_PAD), 1)

    for h in range(HEADS):
        sl = slice(h * HEAD_PAD, (h + 1) * HEAD_PAD)
        li_c, lf_c = gc[:, h:h + 1], lfc[:, HEADS + h:HEADS + h + 1]
        li_r, lf_r = gr[h:h + 1, :], lfr[HEADS + h:HEADS + h + 1, :]
        m_old = m_scr[h]
        b_c = jnp.sum(jnp.where(tri, lf_r, 0.0), axis=1, keepdims=True)
        b_r = jnp.sum(jnp.where(tri_t, lf_c, 0.0), axis=0, keepdims=True)
        dmat = jnp.where(tri, b_c - b_r + li_r, -jnp.inf)
        inter = b_c + m_old
        m_row = jnp.maximum(inter, jnp.max(dmat, axis=1, keepdims=True))
        wgt = jnp.exp(dmat - m_row)
        a = jnp.exp(inter - m_row)
        qh = q[:, sl].astype(BF16)
        kh = k[:, sl]
        vh = jnp.where(lane == HEAD_DIM, jnp.ones((), BF16), v[:, sl])
        sqk = lax.dot_general(qh, kh.astype(BF16), (((1,), (1,)), ((), ())),
                              preferred_element_type=F32) * wgt
        c_old = c_scr[h]
        num = (jnp.dot(sqk.astype(BF16), vh, preferred_element_type=F32)
               + a * jnp.dot(qh, c_old.astype(BF16), preferred_element_type=F32))
        den = num[:, HEAD_DIM:HEAD_DIM + 1]
        hj = num / jnp.maximum(jnp.abs(den), jnp.exp(-m_row))
        o_ref[0, 0, :, sl] = jnp.where(lane < HEAD_DIM, hj, 0.0).astype(o_ref.dtype)

        b_last = jnp.sum(lf_c, axis=0, keepdims=True)
        g = b_last - b_c + li_c
        m_new = jnp.maximum(b_last + m_old, jnp.max(g, axis=0, keepdims=True))
        kw = (kh * jnp.exp(g - m_new)).astype(BF16)
        decay = jnp.exp(b_last + m_old - m_new)
        c_scr[h] = decay * c_old + lax.dot_general(kw, vh, (((0,), (0,)), ((), ())),
                                                   preferred_element_type=F32)
        m_scr[h] = m_new


def _mlstm(proj, gates_col, gates_row, cwq, cbq, cwk, cbk, batch, seq):
    L = MLSTM_CHUNK
    nc = seq // L
    hb = L // HALO_ROWS
    nhalo = seq // HALO_ROWS
    view = proj.reshape(proj.shape[0], batch, seq, WIDTH_PAD)

    def pos(d, c):
        return jnp.where(d == 0, c, nc - 1 - c)

    def main(which):
        return pl.BlockSpec((1, 1, L, WIDTH_PAD), lambda b, d, c: (which, b, pos(d, c), 0))

    def halo(which, after):
        def imap(b, d, c):
            p = pos(d, c)
            blk = (p + 1) * hb if after else p * hb - 1
            return (which, b, jnp.clip(blk, 0, nhalo - 1), 0)
        return pl.BlockSpec((1, 1, HALO_ROWS, WIDTH_PAD), imap)

    def const(shape):
        return pl.BlockSpec(shape, lambda b, d, c: (0,) * len(shape))

    out = pl.pallas_call(
        functools.partial(_mlstm_kernel, nc=nc),
        grid=(batch, 2, nc),
        in_specs=[main(0), main(1), main(2), halo(0, False), halo(0, True), halo(1, False), halo(1, True),
                  pl.BlockSpec((1, 1, L, 2 * HEADS), lambda b, d, c: (d, b, pos(d, c), 0)),
                  pl.BlockSpec((1, 1, 2 * HEADS, L), lambda b, d, c: (d, b, 0, pos(d, c))),
                  const((3, WIDTH_PAD)), const((1, WIDTH_PAD)), const((3, WIDTH_PAD)), const((1, WIDTH_PAD))],
        out_specs=pl.BlockSpec((1, 1, L, WIDTH_PAD), lambda b, d, c: (d, b, pos(d, c), 0)),
        out_shape=jax.ShapeDtypeStruct((2, batch, seq, WIDTH_PAD), BF16),
        scratch_shapes=[pltpu.VMEM((HEADS, HEAD_PAD, HEAD_PAD), F32), pltpu.VMEM((HEADS, 1, 1), F32)],
        compiler_params=_cparams(("parallel", "arbitrary", "arbitrary")),
        name="mlstm",
    )(view, view, view, view, view, view, view, gates_col, gates_row, cwq, cbq, cwk, cbk)
    return out.reshape(2, batch * seq, WIDTH_PAD)


def _merge_kernel(x_ref, o0_ref, o1_ref, o2_ref, l0_ref, l1_ref, l2_ref, hf_ref, hb_ref, om_ref,
                  xq_ref, g0_ref, g1_ref, g2_ref, xk_ref, xv_ref, hg_ref, wb0_ref, wb1_ref, wb2_ref,
                  wo_ref, out_ref):
    l0, l1, l2 = l0_ref[...], l1_ref[...], l2_ref[...]
    lm = jnp.maximum(jnp.maximum(l0, l1), l2)
    e0, e1, e2 = jnp.exp(l0 - lm), jnp.exp(l1 - lm), jnp.exp(l2 - lm)
    inv = 1.0 / (e0 + e1 + e2)
    p_att = None
    for gi, (o_ref, e) in enumerate(((o0_ref, e0), (o1_ref, e1), (o2_ref, e2))):
        a = (o_ref[...].astype(F32) * (e * inv)).astype(BF16)
        t = jnp.dot(a, wb0_ref[gi * ATT_GROUP_WIDTH:(gi + 1) * ATT_GROUP_WIDTH, :],
                    preferred_element_type=F32)
        p_att = t if p_att is None else p_att + t

    hs = hf_ref[0].astype(F32) + hb_ref[0].astype(F32)
    parts = []
    for h in range(HEADS):
        hh = hs[:, h * HEAD_PAD:(h + 1) * HEAD_PAD]
        ms = jnp.sum(hh * hh, axis=1, keepdims=True) * (1.0 / HEAD_DIM)
        parts.append(hh * lax.rsqrt(ms + EPS))
    hm = (jnp.concatenate(parts, axis=1) * hg_ref[...] * om_ref[0].astype(F32)).astype(BF16)
    p_ml = jnp.dot(hm, wb1_ref[...], preferred_element_type=F32)

    xq = xq_ref[0]
    xk = xk_ref[0]
    xv = xv_ref[0]
    parts = []
    for h in range(HEADS):
        sl = slice(h * HEAD_PAD, (h + 1) * HEAD_PAD)
        s = lax.dot_general(xq[:, sl], xk[:, sl], (((1,), (1,)), ((), ())), preferred_element_type=F32)
        m = jnp.max(s, axis=1, keepdims=True)
        p = jnp.exp(s - m)
        l = jnp.sum(p, axis=1, keepdims=True)
        parts.append(jnp.dot(p.astype(BF16), xv[:, sl], preferred_element_type=F32) * (1.0 / l))
    xo = jnp.concatenate(parts, axis=1).astype(BF16)
    p_x = jnp.dot(xo, wb2_ref[...], preferred_element_type=F32)

    merged = (g0_ref[0].astype(F32) * p_att + g1_ref[0].astype(F32) * p_ml + g2_ref[0].astype(F32) * p_x)
    out_ref[...] = x_ref[...] + jnp.dot(merged.astype(BF16), wo_ref[...], preferred_element_type=F32)


def _merge(x, att, hdir, proj, memkv, hg, wb0, wb1, wb2, wo, batch, seq):
    n_tok = batch * seq
    t = MERGE_TILE
    tps = seq // t
    tok = lambda w: pl.BlockSpec((t, w), lambda i: (i, 0))
    chunk = lambda c: pl.BlockSpec((1, t, WIDTH_PAD), lambda i: (c, i, 0))
    const = lambda shape: pl.BlockSpec(shape, lambda i: (0,) * len(shape))
    (o0, l0), (o1, l1), (o2, l2) = att
    return pl.pallas_call(
        _merge_kernel,
        grid=(n_tok // t,),
        in_specs=[tok(D_MODEL),
                  tok(ATT_GROUP_WIDTH), tok(ATT_GROUP_WIDTH), tok(ATT_GROUP_WIDTH),
                  tok(ATT_GROUP_WIDTH), tok(ATT_GROUP_WIDTH), tok(ATT_GROUP_WIDTH),
                  chunk(0), chunk(1),
                  chunk(3), chunk(4), chunk(5), chunk(6), chunk(7),
                  pl.BlockSpec((1, N_MEM_ROWS, WIDTH_PAD), lambda i: (0, i // tps, 0)),
                  pl.BlockSpec((1, N_MEM_ROWS, WIDTH_PAD), lambda i: (1, i // tps, 0)),
                  const((1, WIDTH_PAD)), const((ATT_WIDTH, D_MODEL)), const((WIDTH_PAD, D_MODEL)),
                  const((WIDTH_PAD, D_MODEL)), const((D_MODEL, D_MODEL))],
        out_specs=tok(D_MODEL),
        out_shape=jax.ShapeDtypeStruct((n_tok, D_MODEL), F32),
        compiler_params=_cparams(("parallel",)),
        name="merge",
    )(x, o0, o1, o2, l0, l1, l2, hdir, hdir, proj, proj, proj, proj, proj, memkv, memkv,
      hg, wb0, wb1, wb2, wo)


def _gelu_tanh(x):
    return 0.5 * x * (1.0 + jnp.tanh(math.sqrt(2.0 / math.pi) * (x + 0.044715 * (x * x * x))))


def _ffn_kernel(x_ref, xp_ref, xn_ref, g_ref, wa_ref, wv_ref, cwa_ref, cwv_ref, cba_ref, cbv_ref,
                wd_ref, o_ref, h_scr, acc_scr, *, tiles_per_seq):
    t = x_ref.shape[0]
    i = pl.program_id(0)
    j = pl.program_id(1)

    @pl.when(j == 0)
    def _():
        g = g_ref[...]
        h_scr[0:t] = _rmsnorm_rows(x_ref[...], g).astype(BF16)
        h_scr[t:t + HALO_ROWS] = _rmsnorm_rows(xp_ref[...], g).astype(BF16)
        h_scr[t + HALO_ROWS:t + 2 * HALO_ROWS] = _rmsnorm_rows(xn_ref[...], g).astype(BF16)
        acc_scr[...] = jnp.zeros_like(acc_scr)

    h = h_scr[...]
    pos = i % tiles_per_seq
    has_prev = (pos > 0).astype(F32)
    has_next = (pos < tiles_per_seq - 1).astype(F32)
    rows = lax.broadcasted_iota(jnp.int32, (t, 1), 0)

    def conv_branch(w_ref, cw_ref, cb_ref):
        zz = jnp.dot(h, w_ref[...], preferred_element_type=F32)
        z = zz[:t]
        prev_row = zz[t + HALO_ROWS - 1:t + HALO_ROWS] * has_prev
        next_row = zz[t + HALO_ROWS:t + HALO_ROWS + 1] * has_next
        zm = jnp.where(rows == 0, prev_row, pltpu.roll(z, 1, 0))
        zp = jnp.where(rows == t - 1, next_row, pltpu.roll(z, t - 1, 0))
        cw = cw_ref[...]
        return cw[0:1] * zm + cw[1:2] * z + cw[2:3] * zp + cb_ref[...]

    act = _gelu_tanh(conv_branch(wa_ref, cwa_ref, cba_ref)) * conv_branch(wv_ref, cwv_ref, cbv_ref)
    acc_scr[...] += jnp.dot(act.astype(BF16), wd_ref[...], preferred_element_type=F32)

    @pl.when(j == pl.num_programs(1) - 1)
    def _():
        o_ref[...] = x_ref[...] + acc_scr[...]


def _ffn(x, g, w_up, conv_w, conv_b, w_down, seq):
    n_tok, d = x.shape
    t = FFN_TILE
    fc = FFN_CHUNK
    nf = FFN_DIM // fc
    hb = t // HALO_ROWS
    nhalo = n_tok // HALO_ROWS
    return pl.pallas_call(
        functools.partial(_ffn_kernel, tiles_per_seq=seq // t),
        grid=(n_tok // t, nf),
        in_specs=[pl.BlockSpec((t, d), lambda i, j: (i, 0)),
                  pl.BlockSpec((HALO_ROWS, d), lambda i, j: (jnp.maximum(i * hb - 1, 0), 0)),
                  pl.BlockSpec((HALO_ROWS, d), lambda i, j: (jnp.minimum((i + 1) * hb, nhalo - 1), 0)),
                  pl.BlockSpec((1, d), lambda i, j: (0, 0)),
                  pl.BlockSpec((d, fc), lambda i, j: (0, j)),
                  pl.BlockSpec((d, fc), lambda i, j: (0, nf + j)),
                  pl.BlockSpec((3, fc), lambda i, j: (0, j)),
                  pl.BlockSpec((3, fc), lambda i, j: (0, nf + j)),
                  pl.BlockSpec((1, fc), lambda i, j: (0, j)),
                  pl.BlockSpec((1, fc), lambda i, j: (0, nf + j)),
                  pl.BlockSpec((fc, d), lambda i, j: (j, 0))],
        out_specs=pl.BlockSpec((t, d), lambda i, j: (i, 0)),
        out_shape=jax.ShapeDtypeStruct((n_tok, d), F32),
        scratch_shapes=[pltpu.VMEM((t + 2 * HALO_ROWS, d), BF16), pltpu.VMEM((t, d), F32)],
        compiler_params=_cparams(("parallel", "arbitrary")),
        name="ffn",
    )(x, x, x, g.reshape(1, d), w_up, w_up, conv_w, conv_w, conv_b, conv_b, w_down)


def _pad_heads_cols(w):
    lead = w.shape[:-1]
    w = w.reshape(lead + (HEADS, HEAD_DIM))
    w = jnp.pad(w, [(0, 0)] * len(lead) + [(0, 0), (0, HEAD_PAD - HEAD_DIM)])
    return w.reshape(lead + (WIDTH_PAD,))


def _pad_heads_rows(w):
    return _pad_heads_cols(w.T).T


def _prep_layer(p):
    w_in = p["w_in"]
    a = ATT_WIDTH
    segs = {}
    off = 0
    for name, size in (("aq", a), ("ak", a), ("av", a), ("mq", WIDTH), ("mk", WIDTH), ("mv", WIDTH),
                       ("mo", WIDTH), ("mif", N_GATES), ("xq", WIDTH), ("gpre", 3 * D_MODEL)):
        segs[name] = w_in[:, off:off + size]
        off += size
    out = {}
    out["w_att"] = jnp.concatenate([segs["aq"], segs["ak"], segs["av"]], axis=1).astype(BF16)
    q_gain = jnp.tile(p["att_q_g"], ATT_HEADS) * (ATT_HEAD_DIM ** -0.5)
    k_gain = jnp.tile(p["att_k_g"], ATT_HEADS)
    out["att_gains"] = jnp.stack([q_gain, k_gain, jnp.ones_like(k_gain)])[:, None, :]
    out["w_main"] = jnp.concatenate(
        [_pad_heads_cols(segs[n]) for n in ("mq", "mk", "mv", "mo", "xq")] + [segs["gpre"]],
        axis=1).astype(BF16)
    xq_gain = _pad_heads_cols(jnp.tile(p["xatt_q_g"], HEADS)) * (HEAD_DIM ** -0.5)
    ones = jnp.ones((WIDTH_PAD,), F32)
    out["main_gains"] = jnp.stack([ones] * 4 + [xq_gain] + [ones] * 3)[:, None, :]
    out["w_gates"] = jnp.pad(segs["mif"], ((0, 0), (0, GATE_PAD - N_GATES))).astype(BF16)
    out["b_gates"] = jnp.pad(p["mlstm_gate_b"], (0, GATE_PAD - N_GATES))[None, :]
    cw, cb = p["mlstm_conv_w"], p["mlstm_conv_b"]
    out["cwq"], out["cwk"] = _pad_heads_cols(cw[:, :WIDTH]), _pad_heads_cols(cw[:, WIDTH:])
    out["cbq"], out["cbk"] = _pad_heads_cols(cb[None, :WIDTH]), _pad_heads_cols(cb[None, WIDTH:])
    wkv = p["w_mem_kv"]
    out["w_mem"] = jnp.concatenate([_pad_heads_cols(wkv[:, :WIDTH]), _pad_heads_cols(wkv[:, WIDTH:])],
                                   axis=1).astype(BF16)
    xk_gain = _pad_heads_cols(jnp.tile(p["xatt_k_g"], HEADS))
    out["mem_gains"] = jnp.stack([xk_gain, ones])[:, None, :]
    out["hg"] = _pad_heads_cols(p["mlstm_h_g"])[None, :]
    wb = p["w_branch"]
    out["wb0"] = wb[0].astype(BF16)
    out["wb1"] = _pad_heads_rows(wb[1]).astype(BF16)
    out["wb2"] = _pad_heads_rows(wb[2]).astype(BF16)
    out["wo"] = p["w_out"].astype(BF16)
    out["w_up"] = p["w_up"].astype(BF16)
    out["w_down"] = p["w_down"].astype(BF16)
    out["ffn_cw"] = p["ffn_conv_w"]
    out["ffn_cb"] = p["ffn_conv_b"][None, :]
    out["norm_mix_g"], out["norm_mem_g"], out["norm_ffn_g"] = p["norm_mix_g"], p["norm_mem_g"], p["norm_ffn_g"]
    return out


ATT_MODES = ("norm64", "norm64", "raw")
MAIN_MODES = ("raw", "raw", "raw", "sig", "norm192", "sig", "sig", "sig")
MEM_MODES = ("norm192", "raw")


def _layer(x, mem, w, batch, seq):
    qkv = _proj(x, w["norm_mix_g"], w["w_att"], w["att_gains"], ATT_MODES, PROJ_TILE)
    proj, gates = _proj(x, w["norm_mix_g"], w["w_main"], w["main_gains"], MAIN_MODES, PROJ_TILE,
                        wg=w["w_gates"], bg=w["b_gates"])
    memkv = _proj(mem, w["norm_mem_g"], w["w_mem"], w["mem_gains"], MEM_MODES, mem.shape[0] // batch)
    att = [_attention_group(qkv, gi, batch, seq) for gi in range(len(ATT_GROUPS))]
    gt = gates[:, :N_GATES].reshape(batch, seq, 2, 2 * HEADS)
    gates_col = gt.transpose(2, 0, 1, 3)
    gates_row = gt.transpose(2, 0, 3, 1)
    hdir = _mlstm(proj, gates_col, gates_row, w["cwq"], w["cbq"], w["cwk"], w["cbk"], batch, seq)
    x = _merge(x, att, hdir, proj, memkv, w["hg"], w["wb0"], w["wb1"], w["wb2"], w["wo"], batch, seq)
    return _ffn(x, w["norm_ffn_g"], w["w_up"], w["ffn_cw"], w["ffn_cb"], w["w_down"], seq)


def _trunk(x, mem, layers):
    batch, seq, d = x.shape
    xf = x.reshape(batch * seq, d)
    memf = mem.reshape(batch * mem.shape[1], d)
    for w in layers:
        xf = _layer(xf, memf, w, batch, seq)
    return xf.reshape(batch, seq, d)


def kernel(x_prompt, x_sample, mem_prompt, mem_sample, norm_mix_g, norm_mem_g, w_in, mlstm_conv_w, mlstm_conv_b, mlstm_gate_b, att_q_g, att_k_g, xatt_q_g, xatt_k_g, w_mem_kv, mlstm_h_g, w_branch, w_out, norm_ffn_g, w_up, ffn_conv_w, ffn_conv_b, w_down):
    params = dict(norm_mix_g=norm_mix_g, norm_mem_g=norm_mem_g, w_in=w_in, mlstm_conv_w=mlstm_conv_w,
                  mlstm_conv_b=mlstm_conv_b, mlstm_gate_b=mlstm_gate_b, att_q_g=att_q_g, att_k_g=att_k_g,
                  xatt_q_g=xatt_q_g, xatt_k_g=xatt_k_g, w_mem_kv=w_mem_kv, mlstm_h_g=mlstm_h_g,
                  w_branch=w_branch, w_out=w_out, norm_ffn_g=norm_ffn_g, w_up=w_up,
                  ffn_conv_w=ffn_conv_w, ffn_conv_b=ffn_conv_b, w_down=w_down)
    layers = [_prep_layer({k: v[l] for k, v in params.items()}) for l in range(DEPTH)]
    return (_trunk(x_prompt, mem_prompt, layers), _trunk(x_sample, mem_sample, layers))
```

```python
import functools
import math

import jax
import jax.numpy as jnp
from jax import lax
from jax.experimental import pallas as pl
from jax.experimental.pallas import tpu as pltpu

F32 = jnp.float32
BF16 = jnp.bfloat16

D_MODEL = 1024
DEPTH = 4
EPS = 1e-6
NEG_INF = -1e30

ATT_GROUPS = ((128, 1), (512, 4), (2048, 16))
ATT_HEADS_PER_GROUP = 4
ATT_HEAD_DIM = 64
ATT_HEADS = ATT_HEADS_PER_GROUP * len(ATT_GROUPS)
ATT_WIDTH = ATT_HEADS * ATT_HEAD_DIM
ATT_GROUP_WIDTH = ATT_HEADS_PER_GROUP * ATT_HEAD_DIM
ATT_RADIUS = 64
ATT_TQ = 128
ATT_SUPER = 2048

HEADS = 4
HEAD_DIM = 192
HEAD_PAD = 256
WIDTH = HEADS * HEAD_DIM
WIDTH_PAD = HEADS * HEAD_PAD
MLSTM_CHUNK = 256
N_MEM_ROWS = 256
N_GATES = 4 * HEADS
GATE_PAD = 128
LANES = 128
HALO_ROWS = 16

FFN_DIM = 2816
FFN_CHUNK = 1408

PROJ_TILE = 1024
MERGE_TILE = 512
FFN_TILE = 512
VMEM_LIMIT_BYTES = 56 * 1024 * 1024


def _cparams(sem):
    return pltpu.CompilerParams(dimension_semantics=sem, vmem_limit_bytes=VMEM_LIMIT_BYTES)


def _rmsnorm_rows(x, g):
    ms = jnp.mean(x * x, axis=-1, keepdims=True)
    return x * lax.rsqrt(ms + EPS) * g


def _sigmoid(y):
    return 1.0 / (1.0 + jnp.exp(-y))


def _log_sigmoid(y):
    return -(jnp.maximum(-y, 0.0) + jnp.log(1.0 + jnp.exp(-jnp.abs(y))))


def _conv3_rows(z, prev_row, next_row, cw, cb):
    t = z.shape[0]
    rows = lax.broadcasted_iota(jnp.int32, (t, 1), 0)
    zm = jnp.where(rows == 0, prev_row, pltpu.roll(z, 1, 0))
    zp = jnp.where(rows == t - 1, next_row, pltpu.roll(z, t - 1, 0))
    return cw[0:1] * zm + cw[1:2] * z + cw[2:3] * zp + cb


def _proj_kernel(*refs, modes, with_conv, tiles_per_seq):
    if with_conv:
        (x_ref, xp_ref, xn_ref, g_ref, w_ref, gain_ref, cw_ref, cb_ref, wg_ref, bg_ref,
         o_ref, gates_ref, h_scr) = refs
    else:
        x_ref, g_ref, w_ref, gain_ref, o_ref, h_scr = refs
    t = x_ref.shape[0]
    i = pl.program_id(0)
    j = pl.program_id(1)

    @pl.when(j == 0)
    def _():
        g = g_ref[...]
        h = _rmsnorm_rows(x_ref[...], g).astype(BF16)
        h_scr[0:t] = h
        if with_conv:
            h_scr[t:t + HALO_ROWS] = _rmsnorm_rows(xp_ref[...], g).astype(BF16)
            h_scr[t + HALO_ROWS:t + 2 * HALO_ROWS] = _rmsnorm_rows(xn_ref[...], g).astype(BF16)
            gates_ref[...] = jnp.dot(h, wg_ref[...], preferred_element_type=F32) + bg_ref[...]

    y_all = jnp.dot(h_scr[...], w_ref[...], preferred_element_type=F32)
    y = y_all[:t]
    wc = y.shape[1]

    for mode in sorted(set(modes)):
        pred = functools.reduce(jnp.logical_or, [j == k for k, m in enumerate(modes) if m == mode])

        @pl.when(pred)
        def _(mode=mode):
            if mode == "raw":
                r = y
            elif mode == "sig":
                r = _sigmoid(y)
            elif mode == "conv_silu":
                pos = i % tiles_per_seq
                prev_row = y_all[t + HALO_ROWS - 1:t + HALO_ROWS] * (pos > 0).astype(F32)
                next_row = y_all[t + HALO_ROWS:t + HALO_ROWS + 1] * (pos < tiles_per_seq - 1).astype(F32)
                u = _conv3_rows(y, prev_row, next_row, cw_ref[0], cb_ref[0])
                r = u * _sigmoid(u) * gain_ref[0]
            else:
                parts = []
                for c in range(0, wc, HEAD_PAD):
                    yh = y[:, c:c + HEAD_PAD]
                    ms = jnp.sum(yh * yh, axis=1, keepdims=True) * (1.0 / HEAD_DIM)
                    parts.append(yh * lax.rsqrt(ms + EPS))
                r = jnp.concatenate(parts, axis=1) * gain_ref[0]
            o_ref[0] = r.astype(o_ref.dtype)


def _proj(x, g, w, gains, modes, tile, seq=None, conv=None):
    n_tok, d = x.shape
    n_chunks = len(modes)
    wc = w.shape[1] // n_chunks
    with_conv = conv is not None
    hb = tile // HALO_ROWS
    nhalo = n_tok // HALO_ROWS
    in_specs = [pl.BlockSpec((tile, d), lambda i, j: (i, 0))]
    args = [x]
    if with_conv:
        in_specs += [pl.BlockSpec((HALO_ROWS, d), lambda i, j: (jnp.maximum(i * hb - 1, 0), 0)),
                     pl.BlockSpec((HALO_ROWS, d), lambda i, j: (jnp.minimum((i + 1) * hb, nhalo - 1), 0))]
        args += [x, x]
    in_specs += [pl.BlockSpec((1, d), lambda i, j: (0, 0)),
                 pl.BlockSpec((d, wc), lambda i, j: (0, j)),
                 pl.BlockSpec((1, 1, wc), lambda i, j: (j, 0, 0))]
    args += [g.reshape(1, d), w, gains]
    out_shape = [jax.ShapeDtypeStruct((n_chunks, n_tok, wc), BF16)]
    out_specs = [pl.BlockSpec((1, tile, wc), lambda i, j: (j, i, 0))]
    rows = tile
    if with_conv:
        cw, cb, wg, bg = conv
        in_specs += [pl.BlockSpec((1, 3, wc), lambda i, j: (j, 0, 0)),
                     pl.BlockSpec((1, 1, wc), lambda i, j: (j, 0, 0)),
                     pl.BlockSpec((d, GATE_PAD), lambda i, j: (0, 0)),
                     pl.BlockSpec((1, GATE_PAD), lambda i, j: (0, 0))]
        args += [cw, cb, wg, bg]
        out_shape.append(jax.ShapeDtypeStruct((n_tok, GATE_PAD), F32))
        out_specs.append(pl.BlockSpec((tile, GATE_PAD), lambda i, j: (i, 0)))
        rows = tile + 2 * HALO_ROWS
    res = pl.pallas_call(
        functools.partial(_proj_kernel, modes=tuple(modes), with_conv=with_conv,
                          tiles_per_seq=(seq // tile if with_conv else 1)),
        grid=(n_tok // tile, n_chunks),
        in_specs=in_specs,
        out_specs=out_specs,
        out_shape=out_shape,
        scratch_shapes=[pltpu.VMEM((rows, d), BF16)],
        compiler_params=_cparams(("parallel", "arbitrary")),
        name="proj",
    )(*args)
    return res if with_conv else res[0]


def _attproj_kernel(x_ref, g_ref, w_ref, gain_ref, bd_ref, o0_ref, o1_ref, o2_ref, h_scr, y_scr):
    t = x_ref.shape[0]
    j = pl.program_id(1)

    @pl.when(j == 0)
    def _():
        h_scr[...] = _rmsnorm_rows(x_ref[...], g_ref[...]).astype(BF16)

    y = jnp.dot(h_scr[...], w_ref[...], preferred_element_type=F32)

    def stash(r):
        for c in range(ATT_WIDTH // LANES):
            y_scr[c] = r[:, c * LANES:(c + 1) * LANES]

    @pl.when(j < 2)
    def _():
        sq = (y * y).astype(BF16)
        ss = jnp.concatenate(
            [jnp.dot(sq[:, c:c + HEAD_PAD], bd_ref[...], preferred_element_type=F32)
             for c in range(0, ATT_WIDTH, HEAD_PAD)], axis=1)
        stash(y * lax.rsqrt(ss * (1.0 / ATT_HEAD_DIM) + EPS) * gain_ref[0])

    @pl.when(j == 2)
    def _():
        stash(y)

    blocks_per_group = ATT_GROUP_WIDTH // LANES
    for gi, o_ref in enumerate((o0_ref, o1_ref, o2_ref)):
        dil = ATT_GROUPS[gi][1]
        for r in range(dil):
            rows = pl.ds(r, t // dil, stride=dil) if dil > 1 else pl.ds(0, t)
            for c in range(blocks_per_group):
                o_ref[0, 0, r, :, c * LANES:(c + 1) * LANES] = (
                    y_scr[gi * blocks_per_group + c, rows, :].astype(BF16))


def _attproj(x, g, w, gains, batch, seq):
    n_tok, d = x.shape
    tile = PROJ_TILE
    tps = seq // tile
    bd = (jnp.arange(HEAD_PAD)[:, None] // ATT_HEAD_DIM
          == jnp.arange(HEAD_PAD)[None, :] // ATT_HEAD_DIM).astype(BF16)
    out_shape, out_specs = [], []
    for _, dil in ATT_GROUPS:
        out_shape.append(jax.ShapeDtypeStruct((3, batch, dil, seq // dil, ATT_GROUP_WIDTH), BF16))
        out_specs.append(pl.BlockSpec((1, 1, dil, tile // dil, ATT_GROUP_WIDTH),
                                      lambda i, j: (j, i // tps, 0, i % tps, 0)))
    return pl.pallas_call(
        _attproj_kernel,
        grid=(n_tok // tile, 3),
        in_specs=[pl.BlockSpec((tile, d), lambda i, j: (i, 0)),
                  pl.BlockSpec((1, d), lambda i, j: (0, 0)),
                  pl.BlockSpec((d, ATT_WIDTH), lambda i, j: (0, j)),
                  pl.BlockSpec((1, 1, ATT_WIDTH), lambda i, j: (j, 0, 0)),
                  pl.BlockSpec((HEAD_PAD, HEAD_PAD), lambda i, j: (0, 0))],
        out_specs=out_specs,
        out_shape=out_shape,
        scratch_shapes=[pltpu.VMEM((tile, d), BF16), pltpu.VMEM((ATT_WIDTH // LANES, tile, LANES), F32)],
        compiler_params=_cparams(("parallel", "arbitrary")),
        name="attproj",
    )(x, g.reshape(1, d), w, gains, bd)


def _att_kernel(q_ref, k_ref, v_ref, kp_ref, kn_ref, vp_ref, vn_ref, o_ref, l_ref,
                kext, vext, ocls, lcls, onat, lnat, *, dil, lc, slopes):
    tq, r, nh, gw = ATT_TQ, ATT_RADIUS, ATT_HEADS_PER_GROUP, ATT_GROUP_WIDTH
    win = tq + 2 * r
    cp = ATT_SUPER // dil
    nub = cp // tq
    base = pl.program_id(1) * cp

    kext[:, 0:tq] = kp_ref[0, 0]
    kext[:, tq:tq + cp] = k_ref[0, 0]
    kext[:, tq + cp:] = kn_ref[0, 0]
    vext[:, 0:tq] = vp_ref[0, 0]
    vext[:, tq:tq + cp] = v_ref[0, 0]
    vext[:, tq + cp:] = vn_ref[0, 0]

    row = lax.broadcasted_iota(jnp.int32, (nh * tq, win), 0)
    col = lax.broadcasted_iota(jnp.int32, (nh * tq, win), 1)
    dist = jnp.abs(col - r - (row & (tq - 1)))
    head = row // tq
    slope = jnp.where(head == 0, slopes[0], jnp.where(head == 1, slopes[1],
                      jnp.where(head == 2, slopes[2], slopes[3])))
    bias = jnp.where(dist <= r, -slope * (dist * dil).astype(F32), NEG_INF)
    col1 = lax.broadcasted_iota(jnp.int32, (1, win), 1)
    lane = lax.broadcasted_iota(jnp.int32, (1, gw), 1)
    in_head = [(lane >= h * ATT_HEAD_DIM) & (lane < (h + 1) * ATT_HEAD_DIM) for h in range(nh)]

    def body(idx, carry):
        cls = idx // nub
        ub = idx % nub
        q = q_ref[0, 0, cls, pl.ds(pl.multiple_of(ub * tq, tq), tq), :]
        qs = jnp.concatenate([jnp.where(m, q, jnp.zeros_like(q)) for m in in_head], axis=0)
        start = pl.multiple_of(ub * tq + tq - r, r)
        kw = kext[cls, pl.ds(start, win), :]
        vw = vext[cls, pl.ds(start, win), :]
        s = lax.dot_general(qs, kw, (((1,), (1,)), ((), ())), preferred_element_type=F32) + bias
        key_pos = base + ub * tq - r + col1
        s = jnp.where((key_pos >= 0) & (key_pos < lc), s, NEG_INF)
        m = jnp.max(s, axis=1, keepdims=True)
        p = jnp.exp(s - m)
        l = jnp.sum(p, axis=1, keepdims=True)
        ost = jnp.dot(p.astype(BF16), vw, preferred_element_type=F32) * (1.0 / l)
        lse = m + jnp.log(l)
        o = jnp.zeros((tq, gw), F32)
        ls = jnp.zeros((tq, gw), F32)
        for h in range(nh):
            o = jnp.where(in_head[h], ost[h * tq:(h + 1) * tq], o)
            ls = jnp.where(in_head[h], lse[h * tq:(h + 1) * tq], ls)
        dst = pl.ds(pl.multiple_of(ub * tq, tq), tq)
        ocls[cls, dst, :] = o
        lcls[cls, dst, :] = ls
        return carry

    lax.fori_loop(0, dil * nub, body, 0)

    if dil == 1:
        o_ref[...] = ocls[0].astype(o_ref.dtype)
        l_ref[...] = lcls[0]
    else:
        for b in range(gw // LANES):
            cols = slice(b * LANES, (b + 1) * LANES)
            for c in range(dil):
                onat[b, pl.ds(c, cp, stride=dil), :] = ocls[c, :, cols]
                lnat[b, pl.ds(c, cp, stride=dil), :] = lcls[c, :, cols]
            o_ref[:, cols] = onat[b].astype(o_ref.dtype)
            l_ref[:, cols] = lnat[b]


def _attention_group(qkv, gi, batch, seq):
    _, dil = ATT_GROUPS[gi]
    n_tok = batch * seq
    lc = seq // dil
    cp = ATT_SUPER // dil
    nsup = seq // ATT_SUPER
    hb = cp // ATT_TQ
    nhalo = lc // ATT_TQ
    gw = ATT_GROUP_WIDTH
    slopes = tuple(2.0 ** (-8.0 * (gi * ATT_HEADS_PER_GROUP + h + 1) / ATT_HEADS)
                   for h in range(ATT_HEADS_PER_GROUP))

    def main(which):
        return pl.BlockSpec((1, 1, dil, cp, gw), lambda b, u: (which, b, 0, u, 0))

    def halo(which, after):
        def imap(b, u):
            blk = (u + 1) * hb if after else u * hb - 1
            return (which, b, 0, jnp.clip(blk, 0, nhalo - 1), 0)
        return pl.BlockSpec((1, 1, dil, ATT_TQ, gw), imap)

    out_spec = pl.BlockSpec((ATT_SUPER, gw), lambda b, u: (b * nsup + u, 0))
    return pl.pallas_call(
        functools.partial(_att_kernel, dil=dil, lc=lc, slopes=slopes),
        grid=(batch, nsup),
        in_specs=[main(0), main(1), main(2), halo(1, False), halo(1, True), halo(2, False), halo(2, True)],
        out_specs=[out_spec, out_spec],
        out_shape=[jax.ShapeDtypeStruct((n_tok, gw), BF16), jax.ShapeDtypeStruct((n_tok, gw), F32)],
        scratch_shapes=[pltpu.VMEM((dil, cp + 2 * ATT_TQ, gw), BF16), pltpu.VMEM((dil, cp + 2 * ATT_TQ, gw), BF16),
                        pltpu.VMEM((dil, cp, gw), F32), pltpu.VMEM((dil, cp, gw), F32),
                        pltpu.VMEM((gw // LANES, ATT_SUPER, LANES), F32),
                        pltpu.VMEM((gw // LANES, ATT_SUPER, LANES), F32)],
        compiler_params=_cparams(("parallel", "parallel")),
        name=f"att_g{gi}",
    )(qkv, qkv, qkv, qkv, qkv, qkv, qkv)


def _split3(x):
    hi = x.astype(BF16)
    r1 = x - hi.astype(F32)
    mid = r1.astype(BF16)
    lo = (r1 - mid.astype(F32)).astype(BF16)
    return hi, mid, lo


def _mlstm_kernel(q_ref, k_ref, v_ref, gc_ref, gr_ref, o_ref, c_scr, m_scr):
    L = MLSTM_CHUNK
    d = pl.program_id(1)

    @pl.when(pl.program_id(2) == 0)
    def _():
        c_scr[...] = jnp.zeros_like(c_scr)
        m_scr[...] = jnp.zeros_like(m_scr)

    q = q_ref[0, 0]
    k = k_ref[0, 0]
    v = v_ref[0, 0]
    gc = gc_ref[0, 0]
    gr = gr_ref[0, 0]

    sign = 1 - 2 * d
    ti = lax.broadcasted_iota(jnp.int32, (L, L), 0)
    si = lax.broadcasted_iota(jnp.int32, (L, L), 1)
    tri = (ti - si) * sign >= 0
    one = jnp.ones((), BF16)
    tri_b = jnp.where(tri, 1.0, 0.0).astype(BF16)
    tri_tb = jnp.where((si - ti) * sign >= 0, 1.0, 0.0).astype(BF16)
    b_col = sum(jnp.dot(tri_b, part, preferred_element_type=F32) for part in _split3(_log_sigmoid(gc)))
    b_row = sum(jnp.dot(part, tri_tb, preferred_element_type=F32) for part in _split3(_log_sigmoid(gr)))
    lane = lax.broadcasted_iota(jnp.int32, (1, HEAD_PAD), 1)

    for h in range(HEADS):
        sl = slice(h * HEAD_PAD, (h + 1) * HEAD_PAD)
        li_c, b_c = gc[:, h:h + 1], b_col[:, HEADS + h:HEADS + h + 1]
        li_r, b_r = gr[h:h + 1, :], b_row[HEADS + h:HEADS + h + 1, :]
        m_old = m_scr[h]
        dmat = jnp.where(tri, b_c - b_r + li_r, -jnp.inf)
        inter = b_c + m_old
        m_row = jnp.maximum(inter, jnp.max(dmat, axis=1, keepdims=True))
        wgt = jnp.exp(dmat - m_row)
        a = jnp.exp(inter - m_row)
        qh = q[:, sl]
        kh = k[:, sl]
        vh = jnp.where(lane == HEAD_DIM, one, v[:, sl])
        sqk = lax.dot_general(qh, kh, (((1,), (1,)), ((), ())), preferred_element_type=F32) * wgt
        c_old = c_scr[h]
        num = (jnp.dot(sqk.astype(BF16), vh, preferred_element_type=F32)
               + a * jnp.dot(qh, c_old.astype(BF16), preferred_element_type=F32))
        den = num[:, HEAD_DIM:HEAD_DIM + 1]
        hj = num / jnp.maximum(jnp.abs(den), jnp.exp(-m_row))
        o_ref[0, 0, :, sl] = jnp.where(lane < HEAD_DIM, hj, 0.0).astype(o_ref.dtype)

        b_last = jnp.where(d == 0, b_c[L - 1:L], b_c[0:1])
        g = b_last - b_c + li_c
        m_new = jnp.maximum(b_last + m_old, jnp.max(g, axis=0, keepdims=True))
        kw = (kh.astype(F32) * jnp.exp(g - m_new)).astype(BF16)
        decay = jnp.exp(b_last + m_old - m_new)
        c_scr[h] = decay * c_old + lax.dot_general(kw, vh, (((0,), (0,)), ((), ())),
                                                   preferred_element_type=F32)
        m_scr[h] = m_new


def _mlstm(proj, gates_col, gates_row, batch, seq):
    L = MLSTM_CHUNK
    nc = seq // L
    view = proj.reshape(proj.shape[0], batch, seq, WIDTH_PAD)

    def pos(d, c):
        return jnp.where(d == 0, c, nc - 1 - c)

    def main(which):
        return pl.BlockSpec((1, 1, L, WIDTH_PAD), lambda b, d, c: (which, b, pos(d, c), 0))

    out = pl.pallas_call(
        _mlstm_kernel,
        grid=(batch, 2, nc),
        in_specs=[main(0), main(1), main(2),
                  pl.BlockSpec((1, 1, L, 2 * HEADS), lambda b, d, c: (d, b, pos(d, c), 0)),
                  pl.BlockSpec((1, 1, 2 * HEADS, L), lambda b, d, c: (d, b, 0, pos(d, c)))],
        out_specs=pl.BlockSpec((1, 1, L, WIDTH_PAD), lambda b, d, c: (d, b, pos(d, c), 0)),
        out_shape=jax.ShapeDtypeStruct((2, batch, seq, WIDTH_PAD), BF16),
        scratch_shapes=[pltpu.VMEM((HEADS, HEAD_PAD, HEAD_PAD), F32), pltpu.VMEM((HEADS, 1, 1), F32)],
        compiler_params=_cparams(("parallel", "arbitrary", "arbitrary")),
        name="mlstm",
    )(view, view, view, gates_col, gates_row)
    return out.reshape(2, batch * seq, WIDTH_PAD)


def _merge_kernel(x_ref, o0_ref, o1_ref, o2_ref, l0_ref, l1_ref, l2_ref, hf_ref, hb_ref, om_ref,
                  xq_ref, g0_ref, g1_ref, g2_ref, xk_ref, xv_ref, hg_ref, wb0_ref, wb1_ref, wb2_ref,
                  wo_ref, out_ref):
    l0, l1, l2 = l0_ref[...], l1_ref[...], l2_ref[...]
    lm = jnp.maximum(jnp.maximum(l0, l1), l2)
    e0, e1, e2 = jnp.exp(l0 - lm), jnp.exp(l1 - lm), jnp.exp(l2 - lm)
    inv = 1.0 / (e0 + e1 + e2)
    p_att = None
    for gi, (o_ref, e) in enumerate(((o0_ref, e0), (o1_ref, e1), (o2_ref, e2))):
        a = (o_ref[...].astype(F32) * (e * inv)).astype(BF16)
        t = jnp.dot(a, wb0_ref[gi * ATT_GROUP_WIDTH:(gi + 1) * ATT_GROUP_WIDTH, :],
                    preferred_element_type=F32)
        p_att = t if p_att is None else p_att + t

    hs = hf_ref[0].astype(F32) + hb_ref[0].astype(F32)
    parts = []
    for h in range(HEADS):
        hh = hs[:, h * HEAD_PAD:(h + 1) * HEAD_PAD]
        ms = jnp.sum(hh * hh, axis=1, keepdims=True) * (1.0 / HEAD_DIM)
        parts.append(hh * lax.rsqrt(ms + EPS))
    hm = (jnp.concatenate(parts, axis=1) * hg_ref[...] * om_ref[0].astype(F32)).astype(BF16)
    p_ml = jnp.dot(hm, wb1_ref[...], preferred_element_type=F32)

    xq = xq_ref[0]
    xk = xk_ref[0]
    xv = xv_ref[0]
    parts = []
    for h in range(HEADS):
        sl = slice(h * HEAD_PAD, (h + 1) * HEAD_PAD)
        s = lax.dot_general(xq[:, sl], xk[:, sl], (((1,), (1,)), ((), ())), preferred_element_type=F32)
        m = jnp.max(s, axis=1, keepdims=True)
        p = jnp.exp(s - m)
        l = jnp.sum(p, axis=1, keepdims=True)
        parts.append(jnp.dot(p.astype(BF16), xv[:, sl], preferred_element_type=F32) * (1.0 / l))
    xo = jnp.concatenate(parts, axis=1).astype(BF16)
    p_x = jnp.dot(xo, wb2_ref[...], preferred_element_type=F32)

    merged = (g0_ref[0].astype(F32) * p_att + g1_ref[0].astype(F32) * p_ml + g2_ref[0].astype(F32) * p_x)
    out_ref[...] = x_ref[...] + jnp.dot(merged.astype(BF16), wo_ref[...], preferred_element_type=F32)


def _merge(x, att, hdir, proj, memkv, hg, wb0, wb1, wb2, wo, batch, seq):
    n_tok = batch * seq
    t = MERGE_TILE
    tps = seq // t
    tok = lambda w: pl.BlockSpec((t, w), lambda i: (i, 0))
    chunk = lambda c: pl.BlockSpec((1, t, WIDTH_PAD), lambda i: (c, i, 0))
    const = lambda shape: pl.BlockSpec(shape, lambda i: (0,) * len(shape))
    (o0, l0), (o1, l1), (o2, l2) = att
    return pl.pallas_call(
        _merge_kernel,
        grid=(n_tok // t,),
        in_specs=[tok(D_MODEL),
                  tok(ATT_GROUP_WIDTH), tok(ATT_GROUP_WIDTH), tok(ATT_GROUP_WIDTH),
                  tok(ATT_GROUP_WIDTH), tok(ATT_GROUP_WIDTH), tok(ATT_GROUP_WIDTH),
                  chunk(0), chunk(1),
                  chunk(3), chunk(4), chunk(5), chunk(6), chunk(7),
                  pl.BlockSpec((1, N_MEM_ROWS, WIDTH_PAD), lambda i: (0, i // tps, 0)),
                  pl.BlockSpec((1, N_MEM_ROWS, WIDTH_PAD), lambda i: (1, i // tps, 0)),
                  const((1, WIDTH_PAD)), const((ATT_WIDTH, D_MODEL)), const((WIDTH_PAD, D_MODEL)),
                  const((WIDTH_PAD, D_MODEL)), const((D_MODEL, D_MODEL))],
        out_specs=tok(D_MODEL),
        out_shape=jax.ShapeDtypeStruct((n_tok, D_MODEL), F32),
        compiler_params=_cparams(("parallel",)),
        name="merge",
    )(x, o0, o1, o2, l0, l1, l2, hdir, hdir, proj, proj, proj, proj, proj, memkv, memkv,
      hg, wb0, wb1, wb2, wo)


def _gelu_tanh(x):
    return 0.5 * x * (1.0 + jnp.tanh(math.sqrt(2.0 / math.pi) * (x + 0.044715 * (x * x * x))))


def _ffn_kernel(x_ref, xp_ref, xn_ref, g_ref, wa_ref, wv_ref, cwa_ref, cwv_ref, cba_ref, cbv_ref,
                wd_ref, o_ref, h_scr, acc_scr, *, tiles_per_seq):
    t = x_ref.shape[0]
    i = pl.program_id(0)
    j = pl.program_id(1)

    @pl.when(j == 0)
    def _():
        g = g_ref[...]
        h_scr[0:t] = _rmsnorm_rows(x_ref[...], g).astype(BF16)
        h_scr[t:t + HALO_ROWS] = _rmsnorm_rows(xp_ref[...], g).astype(BF16)
        h_scr[t + HALO_ROWS:t + 2 * HALO_ROWS] = _rmsnorm_rows(xn_ref[...], g).astype(BF16)
        acc_scr[...] = jnp.zeros_like(acc_scr)

    h = h_scr[...]
    pos = i % tiles_per_seq
    has_prev = (pos > 0).astype(F32)
    has_next = (pos < tiles_per_seq - 1).astype(F32)

    def conv_branch(w_ref, cw_ref, cb_ref):
        zz = jnp.dot(h, w_ref[...], preferred_element_type=F32)
        prev_row = zz[t + HALO_ROWS - 1:t + HALO_ROWS] * has_prev
        next_row = zz[t + HALO_ROWS:t + HALO_ROWS + 1] * has_next
        return _conv3_rows(zz[:t], prev_row, next_row, cw_ref[...], cb_ref[...])

    act = _gelu_tanh(conv_branch(wa_ref, cwa_ref, cba_ref)) * conv_branch(wv_ref, cwv_ref, cbv_ref)
    acc_scr[...] += jnp.dot(act.astype(BF16), wd_ref[...], preferred_element_type=F32)

    @pl.when(j == pl.num_programs(1) - 1)
    def _():
        o_ref[...] = x_ref[...] + acc_scr[...]


def _ffn(x, g, w_up, conv_w, conv_b, w_down, seq):
    n_tok, d = x.shape
    t = FFN_TILE
    fc = FFN_CHUNK
    nf = FFN_DIM // fc
    hb = t // HALO_ROWS
    nhalo = n_tok // HALO_ROWS
    return pl.pallas_call(
        functools.partial(_ffn_kernel, tiles_per_seq=seq // t),
        grid=(n_tok // t, nf),
        in_specs=[pl.BlockSpec((t, d), lambda i, j: (i, 0)),
                  pl.BlockSpec((HALO_ROWS, d), lambda i, j: (jnp.maximum(i * hb - 1, 0), 0)),
                  pl.BlockSpec((HALO_ROWS, d), lambda i, j: (jnp.minimum((i + 1) * hb, nhalo - 1), 0)),
                  pl.BlockSpec((1, d), lambda i, j: (0, 0)),
                  pl.BlockSpec((d, fc), lambda i, j: (0, j)),
                  pl.BlockSpec((d, fc), lambda i, j: (0, nf + j)),
                  pl.BlockSpec((3, fc), lambda i, j: (0, j)),
                  pl.BlockSpec((3, fc), lambda i, j: (0, nf + j)),
                  pl.BlockSpec((1, fc), lambda i, j: (0, j)),
                  pl.BlockSpec((1, fc), lambda i, j: (0, nf + j)),
                  pl.BlockSpec((fc, d), lambda i, j: (j, 0))],
        out_specs=pl.BlockSpec((t, d), lambda i, j: (i, 0)),
        out_shape=jax.ShapeDtypeStruct((n_tok, d), F32),
        scratch_shapes=[pltpu.VMEM((t + 2 * HALO_ROWS, d), BF16), pltpu.VMEM((t, d), F32)],
        compiler_params=_cparams(("parallel", "arbitrary")),
        name="ffn",
    )(x, x, x, g.reshape(1, d), w_up, w_up, conv_w, conv_w, conv_b, conv_b, w_down)


def _pad_heads_cols(w):
    lead = w.shape[:-1]
    w = w.reshape(lead + (HEADS, HEAD_DIM))
    w = jnp.pad(w, [(0, 0)] * len(lead) + [(0, 0), (0, HEAD_PAD - HEAD_DIM)])
    return w.reshape(lead + (WIDTH_PAD,))


def _pad_heads_rows(w):
    return _pad_heads_cols(w.T).T


MAIN_MODES = ("conv_silu", "conv_silu", "raw", "sig", "norm192", "sig", "sig", "sig")
MEM_MODES = ("norm192", "raw")


def _prep_layer(p):
    w_in = p["w_in"]
    a = ATT_WIDTH
    segs = {}
    off = 0
    for name, size in (("aq", a), ("ak", a), ("av", a), ("mq", WIDTH), ("mk", WIDTH), ("mv", WIDTH),
                       ("mo", WIDTH), ("mif", N_GATES), ("xq", WIDTH), ("gpre", 3 * D_MODEL)):
        segs[name] = w_in[:, off:off + size]
        off += size
    out = {}
    out["w_att"] = jnp.concatenate([segs["aq"], segs["ak"], segs["av"]], axis=1).astype(BF16)
    q_gain = jnp.tile(p["att_q_g"], ATT_HEADS) * (ATT_HEAD_DIM ** -0.5)
    k_gain = jnp.tile(p["att_k_g"], ATT_HEADS)
    out["att_gains"] = jnp.stack([q_gain, k_gain, jnp.ones_like(k_gain)])[:, None, :]
    out["w_main"] = jnp.concatenate(
        [_pad_heads_cols(segs[n]) for n in ("mq", "mk", "mv", "mo", "xq")] + [segs["gpre"]],
        axis=1).astype(BF16)
    xq_gain = _pad_heads_cols(jnp.tile(p["xatt_q_g"], HEADS)) * (HEAD_DIM ** -0.5)
    ones = jnp.ones((WIDTH_PAD,), F32)
    zeros = jnp.zeros((WIDTH_PAD,), F32)
    n_main = len(MAIN_MODES)
    out["main_gains"] = jnp.stack([ones * (HEAD_DIM ** -0.5)] + [ones] * 3 + [xq_gain] + [ones] * 3)[:, None, :]
    cw, cb = p["mlstm_conv_w"], p["mlstm_conv_b"]
    zeros3 = jnp.zeros((3, WIDTH_PAD), F32)
    out["main_cw"] = jnp.stack([_pad_heads_cols(cw[:, :WIDTH]), _pad_heads_cols(cw[:, WIDTH:])]
                               + [zeros3] * (n_main - 2))
    out["main_cb"] = jnp.stack([_pad_heads_cols(cb[:WIDTH]), _pad_heads_cols(cb[WIDTH:])]
                               + [zeros] * (n_main - 2))[:, None, :]
    out["w_gates"] = jnp.pad(segs["mif"], ((0, 0), (0, GATE_PAD - N_GATES))).astype(BF16)
    out["b_gates"] = jnp.pad(p["mlstm_gate_b"], (0, GATE_PAD - N_GATES))[None, :]
    wkv = p["w_mem_kv"]
    out["w_mem"] = jnp.concatenate([_pad_heads_cols(wkv[:, :WIDTH]), _pad_heads_cols(wkv[:, WIDTH:])],
                                   axis=1).astype(BF16)
    xk_gain = _pad_heads_cols(jnp.tile(p["xatt_k_g"], HEADS))
    out["mem_gains"] = jnp.stack([xk_gain, ones])[:, None, :]
    out["hg"] = _pad_heads_cols(p["mlstm_h_g"])[None, :]
    wb = p["w_branch"]
    out["wb0"] = wb[0].astype(BF16)
    out["wb1"] = _pad_heads_rows(wb[1]).astype(BF16)
    out["wb2"] = _pad_heads_rows(wb[2]).astype(BF16)
    out["wo"] = p["w_out"].astype(BF16)
    out["w_up"] = p["w_up"].astype(BF16)
    out["w_down"] = p["w_down"].astype(BF16)
    out["ffn_cw"] = p["ffn_conv_w"]
    out["ffn_cb"] = p["ffn_conv_b"][None, :]
    out["norm_mix_g"], out["norm_mem_g"], out["norm_ffn_g"] = p["norm_mix_g"], p["norm_mem_g"], p["norm_ffn_g"]
    return out


def _layer(x, mem, w, batch, seq):
    qkv = _attproj(x, w["norm_mix_g"], w["w_att"], w["att_gains"], batch, seq)
    proj, gates = _proj(x, w["norm_mix_g"], w["w_main"], w["main_gains"], MAIN_MODES, PROJ_TILE, seq=seq,
                        conv=(w["main_cw"], w["main_cb"], w["w_gates"], w["b_gates"]))
    memkv = _proj(mem, w["norm_mem_g"], w["w_mem"], w["mem_gains"], MEM_MODES, N_MEM_ROWS)
    att = [_attention_group(qkv[gi], gi, batch, seq) for gi in range(len(ATT_GROUPS))]
    gt = gates[:, :N_GATES].reshape(batch, seq, 2, 2 * HEADS)
    gates_col = gt.transpose(2, 0, 1, 3)
    gates_row = gt.transpose(2, 0, 3, 1)
    hdir = _mlstm(proj, gates_col, gates_row, batch, seq)
    x = _merge(x, att, hdir, proj, memkv, w["hg"], w["wb0"], w["wb1"], w["wb2"], w["wo"], batch, seq)
    return _ffn(x, w["norm_ffn_g"], w["w_up"], w["ffn_cw"], w["ffn_cb"], w["w_down"], seq)


def _trunk(x, mem, layers):
    batch, seq, d = x.shape
    xf = x.reshape(batch * seq, d)
    memf = mem.reshape(batch * mem.shape[1], d)
    for w in layers:
        xf = _layer(xf, memf, w, batch, seq)
    return xf.reshape(batch, seq, d)


def kernel(x_prompt, x_sample, mem_prompt, mem_sample, norm_mix_g, norm_mem_g, w_in, mlstm_conv_w, mlstm_conv_b, mlstm_gate_b, att_q_g, att_k_g, xatt_q_g, xatt_k_g, w_mem_kv, mlstm_h_g, w_branch, w_out, norm_ffn_g, w_up, ffn_conv_w, ffn_conv_b, w_down):
    params = dict(norm_mix_g=norm_mix_g, norm_mem_g=norm_mem_g, w_in=w_in, mlstm_conv_w=mlstm_conv_w,
                  mlstm_conv_b=mlstm_conv_b, mlstm_gate_b=mlstm_gate_b, att_q_g=att_q_g, att_k_g=att_k_g,
                  xatt_q_g=xatt_q_g, xatt_k_g=xatt_k_g, w_mem_kv=w_mem_kv, mlstm_h_g=mlstm_h_g,
                  w_branch=w_branch, w_out=w_out, norm_ffn_g=norm_ffn_g, w_up=w_up,
                  ffn_conv_w=ffn_conv_w, ffn_conv_b=ffn_conv_b, w_down=w_down)
    layers = [_prep_layer({k: v[l] for k, v in params.items()}) for l in range(DEPTH)]
    return (_trunk(x_prompt, mem_prompt, layers), _trunk(x_sample, mem_sample, layers))
```

```python
import functools
import math

import jax
import jax.numpy as jnp
from jax import lax
from jax.experimental import pallas as pl
from jax.experimental.pallas import tpu as pltpu

F32 = jnp.float32
BF16 = jnp.bfloat16

D_MODEL = 1024
DEPTH = 4
EPS = 1e-6
NEG_INF = -1e30

ATT_GROUPS = ((128, 1), (512, 4), (2048, 16))
ATT_HEADS_PER_GROUP = 4
ATT_HEAD_DIM = 64
ATT_HEADS = ATT_HEADS_PER_GROUP * len(ATT_GROUPS)
ATT_WIDTH = ATT_HEADS * ATT_HEAD_DIM
ATT_GROUP_WIDTH = ATT_HEADS_PER_GROUP * ATT_HEAD_DIM
ATT_RADIUS = 64
ATT_TQ = 128
ATT_SUPER = 2048

HEADS = 4
HEAD_DIM = 192
HEAD_PAD = 256
WIDTH = HEADS * HEAD_DIM
WIDTH_PAD = HEADS * HEAD_PAD
MLSTM_CHUNK = 256
N_MEM_ROWS = 256
N_GATES = 4 * HEADS
GATE_PAD = 128
LANES = 128
HALO_ROWS = 16

FFN_DIM = 2816
FFN_CHUNK = 1408

PROJ_TILE = 1024
MERGE_TILE = 512
FFN_TILE = 512
VMEM_LIMIT_BYTES = 56 * 1024 * 1024


def _cparams(sem):
    return pltpu.CompilerParams(dimension_semantics=sem, vmem_limit_bytes=VMEM_LIMIT_BYTES)


def _rmsnorm_rows(x, g):
    ms = jnp.mean(x * x, axis=-1, keepdims=True)
    return x * lax.rsqrt(ms + EPS) * g


def _sigmoid(y):
    return 1.0 / (1.0 + jnp.exp(-y))


def _log_sigmoid(y):
    return -(jnp.maximum(-y, 0.0) + jnp.log(1.0 + jnp.exp(-jnp.abs(y))))


def _conv3_rows(z, prev_row, next_row, cw, cb):
    t = z.shape[0]
    rows = lax.broadcasted_iota(jnp.int32, (t, 1), 0)
    zm = jnp.where(rows == 0, prev_row, pltpu.roll(z, 1, 0))
    zp = jnp.where(rows == t - 1, next_row, pltpu.roll(z, t - 1, 0))
    return cw[0:1] * zm + cw[1:2] * z + cw[2:3] * zp + cb


def _proj_kernel(*refs, modes, with_conv, tiles_per_seq):
    if with_conv:
        (x_ref, xp_ref, xn_ref, g_ref, w_ref, gain_ref, cw_ref, cb_ref, wg_ref, bg_ref,
         o_ref, gates_ref, h_scr) = refs
    else:
        x_ref, g_ref, w_ref, gain_ref, o_ref, h_scr = refs
    t = x_ref.shape[0]
    i = pl.program_id(0)
    j = pl.program_id(1)

    @pl.when(j == 0)
    def _():
        g = g_ref[...]
        h = _rmsnorm_rows(x_ref[...], g).astype(BF16)
        h_scr[0:t] = h
        if with_conv:
            h_scr[t:t + HALO_ROWS] = _rmsnorm_rows(xp_ref[...], g).astype(BF16)
            h_scr[t + HALO_ROWS:t + 2 * HALO_ROWS] = _rmsnorm_rows(xn_ref[...], g).astype(BF16)
            gates_ref[...] = jnp.dot(h, wg_ref[...], preferred_element_type=F32) + bg_ref[...]

    y_all = jnp.dot(h_scr[...], w_ref[...], preferred_element_type=F32)
    y = y_all[:t]
    wc = y.shape[1]

    for mode in sorted(set(modes)):
        pred = functools.reduce(jnp.logical_or, [j == k for k, m in enumerate(modes) if m == mode])

        @pl.when(pred)
        def _(mode=mode):
            if mode == "raw":
                r = y
            elif mode == "sig":
                r = _sigmoid(y)
            elif mode == "conv_silu":
                pos = i % tiles_per_seq
                prev_row = y_all[t + HALO_ROWS - 1:t + HALO_ROWS] * (pos > 0).astype(F32)
                next_row = y_all[t + HALO_ROWS:t + HALO_ROWS + 1] * (pos < tiles_per_seq - 1).astype(F32)
                u = _conv3_rows(y, prev_row, next_row, cw_ref[0], cb_ref[0])
                r = u * _sigmoid(u) * gain_ref[0]
            else:
                parts = []
                for c in range(0, wc, HEAD_PAD):
                    yh = y[:, c:c + HEAD_PAD]
                    ms = jnp.sum(yh * yh, axis=1, keepdims=True) * (1.0 / HEAD_DIM)
                    parts.append(yh * lax.rsqrt(ms + EPS))
                r = jnp.concatenate(parts, axis=1) * gain_ref[0]
            o_ref[0] = r.astype(o_ref.dtype)


def _proj(x, g, w, gains, modes, tile, seq=None, conv=None):
    n_tok, d = x.shape
    n_chunks = len(modes)
    wc = w.shape[1] // n_chunks
    with_conv = conv is not None
    hb = tile // HALO_ROWS
    nhalo = n_tok // HALO_ROWS
    in_specs = [pl.BlockSpec((tile, d), lambda i, j: (i, 0))]
    args = [x]
    if with_conv:
        in_specs += [pl.BlockSpec((HALO_ROWS, d), lambda i, j: (jnp.maximum(i * hb - 1, 0), 0)),
                     pl.BlockSpec((HALO_ROWS, d), lambda i, j: (jnp.minimum((i + 1) * hb, nhalo - 1), 0))]
        args += [x, x]
    in_specs += [pl.BlockSpec((1, d), lambda i, j: (0, 0)),
                 pl.BlockSpec((d, wc), lambda i, j: (0, j)),
                 pl.BlockSpec((1, 1, wc), lambda i, j: (j, 0, 0))]
    args += [g.reshape(1, d), w, gains]
    out_shape = [jax.ShapeDtypeStruct((n_chunks, n_tok, wc), BF16)]
    out_specs = [pl.BlockSpec((1, tile, wc), lambda i, j: (j, i, 0))]
    rows = tile
    if with_conv:
        cw, cb, wg, bg = conv
        in_specs += [pl.BlockSpec((1, 3, wc), lambda i, j: (j, 0, 0)),
                     pl.BlockSpec((1, 1, wc), lambda i, j: (j, 0, 0)),
                     pl.BlockSpec((d, GATE_PAD), lambda i, j: (0, 0)),
                     pl.BlockSpec((1, GATE_PAD), lambda i, j: (0, 0))]
        args += [cw, cb, wg, bg]
        out_shape.append(jax.ShapeDtypeStruct((n_tok, GATE_PAD), F32))
        out_specs.append(pl.BlockSpec((tile, GATE_PAD), lambda i, j: (i, 0)))
        rows = tile + 2 * HALO_ROWS
    res = pl.pallas_call(
        functools.partial(_proj_kernel, modes=tuple(modes), with_conv=with_conv,
                          tiles_per_seq=(seq // tile if with_conv else 1)),
        grid=(n_tok // tile, n_chunks),
        in_specs=in_specs,
        out_specs=out_specs,
        out_shape=out_shape,
        scratch_shapes=[pltpu.VMEM((rows, d), BF16)],
        compiler_params=_cparams(("parallel", "arbitrary")),
        name="proj",
    )(*args)
    return res if with_conv else res[0]


def _attproj_kernel(x_ref, g_ref, w_ref, gain_ref, bd_ref, o0_ref, o1_ref, o2_ref, h_scr, y_scr):
    t = x_ref.shape[0]
    j = pl.program_id(1)

    @pl.when(j == 0)
    def _():
        h_scr[...] = _rmsnorm_rows(x_ref[...], g_ref[...]).astype(BF16)

    y = jnp.dot(h_scr[...], w_ref[...], preferred_element_type=F32)

    def stash(r):
        for c in range(ATT_WIDTH // LANES):
            y_scr[c] = r[:, c * LANES:(c + 1) * LANES]

    @pl.when(j < 2)
    def _():
        sq = (y * y).astype(BF16)
        ss = jnp.concatenate(
            [jnp.dot(sq[:, c:c + HEAD_PAD], bd_ref[...], preferred_element_type=F32)
             for c in range(0, ATT_WIDTH, HEAD_PAD)], axis=1)
        stash(y * lax.rsqrt(ss * (1.0 / ATT_HEAD_DIM) + EPS) * gain_ref[0])

    @pl.when(j == 2)
    def _():
        stash(y)

    blocks_per_group = ATT_GROUP_WIDTH // LANES
    for gi, o_ref in enumerate((o0_ref, o1_ref, o2_ref)):
        dil = ATT_GROUPS[gi][1]
        for r in range(dil):
            rows = pl.ds(r, t // dil, stride=dil) if dil > 1 else pl.ds(0, t)
            for c in range(blocks_per_group):
                o_ref[0, 0, r, :, c * LANES:(c + 1) * LANES] = (
                    y_scr[gi * blocks_per_group + c, rows, :].astype(BF16))


def _attproj(x, g, w, gains, batch, seq):
    n_tok, d = x.shape
    tile = PROJ_TILE
    tps = seq // tile
    bd = (jnp.arange(HEAD_PAD)[:, None] // ATT_HEAD_DIM
          == jnp.arange(HEAD_PAD)[None, :] // ATT_HEAD_DIM).astype(BF16)
    out_shape, out_specs = [], []
    for _, dil in ATT_GROUPS:
        out_shape.append(jax.ShapeDtypeStruct((3, batch, dil, seq // dil, ATT_GROUP_WIDTH), BF16))
        out_specs.append(pl.BlockSpec((1, 1, dil, tile // dil, ATT_GROUP_WIDTH),
                                      lambda i, j: (j, i // tps, 0, i % tps, 0)))
    return pl.pallas_call(
        _attproj_kernel,
        grid=(n_tok // tile, 3),
        in_specs=[pl.BlockSpec((tile, d), lambda i, j: (i, 0)),
                  pl.BlockSpec((1, d), lambda i, j: (0, 0)),
                  pl.BlockSpec((d, ATT_WIDTH), lambda i, j: (0, j)),
                  pl.BlockSpec((1, 1, ATT_WIDTH), lambda i, j: (j, 0, 0)),
                  pl.BlockSpec((HEAD_PAD, HEAD_PAD), lambda i, j: (0, 0))],
        out_specs=out_specs,
        out_shape=out_shape,
        scratch_shapes=[pltpu.VMEM((tile, d), BF16), pltpu.VMEM((ATT_WIDTH // LANES, tile, LANES), F32)],
        compiler_params=_cparams(("parallel", "arbitrary")),
        name="attproj",
    )(x, g.reshape(1, d), w, gains, bd)


def _att_kernel(q_ref, k_ref, v_ref, kp_ref, kn_ref, vp_ref, vn_ref, o_ref, l_ref,
                kext, vext, ocls, lcls, onat, lnat, *, dil, lc, slopes):
    tq, r, nh, gw = ATT_TQ, ATT_RADIUS, ATT_HEADS_PER_GROUP, ATT_GROUP_WIDTH
    win = tq + 2 * r
    cp = ATT_SUPER // dil
    nub = cp // tq
    base = pl.program_id(1) * cp

    kext[:, 0:tq] = kp_ref[0, 0]
    kext[:, tq:tq + cp] = k_ref[0, 0]
    kext[:, tq + cp:] = kn_ref[0, 0]
    vext[:, 0:tq] = vp_ref[0, 0]
    vext[:, tq:tq + cp] = v_ref[0, 0]
    vext[:, tq + cp:] = vn_ref[0, 0]

    row = lax.broadcasted_iota(jnp.int32, (nh * tq, win), 0)
    col = lax.broadcasted_iota(jnp.int32, (nh * tq, win), 1)
    dist = jnp.abs(col - r - (row & (tq - 1)))
    head = row // tq
    slope = jnp.where(head == 0, slopes[0], jnp.where(head == 1, slopes[1],
                      jnp.where(head == 2, slopes[2], slopes[3])))
    bias = jnp.where(dist <= r, -slope * (dist * dil).astype(F32), NEG_INF)
    col1 = lax.broadcasted_iota(jnp.int32, (1, win), 1)
    lane = lax.broadcasted_iota(jnp.int32, (1, gw), 1)
    in_head = [(lane >= h * ATT_HEAD_DIM) & (lane < (h + 1) * ATT_HEAD_DIM) for h in range(nh)]

    def body(idx, carry):
        cls = idx // nub
        ub = idx % nub
        q = q_ref[0, 0, cls, pl.ds(pl.multiple_of(ub * tq, tq), tq), :]
        qs = jnp.concatenate([jnp.where(m, q, jnp.zeros_like(q)) for m in in_head], axis=0)
        start = pl.multiple_of(ub * tq + tq - r, r)
        kw = kext[cls, pl.ds(start, win), :]
        vw = vext[cls, pl.ds(start, win), :]
        s = lax.dot_general(qs, kw, (((1,), (1,)), ((), ())), preferred_element_type=F32) + bias
        key_pos = base + ub * tq - r + col1
        s = jnp.where((key_pos >= 0) & (key_pos < lc), s, NEG_INF)
        m = jnp.max(s, axis=1, keepdims=True)
        p = jnp.exp(s - m)
        l = jnp.sum(p, axis=1, keepdims=True)
        ost = jnp.dot(p.astype(BF16), vw, preferred_element_type=F32) * (1.0 / l)
        lse = m + jnp.log(l)
        o = jnp.zeros((tq, gw), F32)
        ls = jnp.zeros((tq, gw), F32)
        for h in range(nh):
            o = jnp.where(in_head[h], ost[h * tq:(h + 1) * tq], o)
            ls = jnp.where(in_head[h], lse[h * tq:(h + 1) * tq], ls)
        dst = pl.ds(pl.multiple_of(ub * tq, tq), tq)
        ocls[cls, dst, :] = o
        lcls[cls, dst, :] = ls
        return carry

    lax.fori_loop(0, dil * nub, body, 0)

    if dil == 1:
        o_ref[...] = ocls[0].astype(o_ref.dtype)
        l_ref[...] = lcls[0]
    else:
        for b in range(gw // LANES):
            cols = slice(b * LANES, (b + 1) * LANES)
            for c in range(dil):
                onat[b, pl.ds(c, cp, stride=dil), :] = ocls[c, :, cols]
                lnat[b, pl.ds(c, cp, stride=dil), :] = lcls[c, :, cols]
            o_ref[:, cols] = onat[b].astype(o_ref.dtype)
            l_ref[:, cols] = lnat[b]


def _attention_group(qkv, gi, batch, seq):
    _, dil = ATT_GROUPS[gi]
    n_tok = batch * seq
    lc = seq // dil
    cp = ATT_SUPER // dil
    nsup = seq // ATT_SUPER
    hb = cp // ATT_TQ
    nhalo = lc // ATT_TQ
    gw = ATT_GROUP_WIDTH
    slopes = tuple(2.0 ** (-8.0 * (gi * ATT_HEADS_PER_GROUP + h + 1) / ATT_HEADS)
                   for h in range(ATT_HEADS_PER_GROUP))

    def main(which):
        return pl.BlockSpec((1, 1, dil, cp, gw), lambda b, u: (which, b, 0, u, 0))

    def halo(which, after):
        def imap(b, u):
            blk = (u + 1) * hb if after else u * hb - 1
            return (which, b, 0, jnp.clip(blk, 0, nhalo - 1), 0)
        return pl.BlockSpec((1, 1, dil, ATT_TQ, gw), imap)

    out_spec = pl.BlockSpec((ATT_SUPER, gw), lambda b, u: (b * nsup + u, 0))
    return pl.pallas_call(
        functools.partial(_att_kernel, dil=dil, lc=lc, slopes=slopes),
        grid=(batch, nsup),
        in_specs=[main(0), main(1), main(2), halo(1, False), halo(1, True), halo(2, False), halo(2, True)],
        out_specs=[out_spec, out_spec],
        out_shape=[jax.ShapeDtypeStruct((n_tok, gw), BF16), jax.ShapeDtypeStruct((n_tok, gw), F32)],
        scratch_shapes=[pltpu.VMEM((dil, cp + 2 * ATT_TQ, gw), BF16), pltpu.VMEM((dil, cp + 2 * ATT_TQ, gw), BF16),
                        pltpu.VMEM((dil, cp, gw), F32), pltpu.VMEM((dil, cp, gw), F32),
                        pltpu.VMEM((gw // LANES, ATT_SUPER, LANES), F32),
                        pltpu.VMEM((gw // LANES, ATT_SUPER, LANES), F32)],
        compiler_params=_cparams(("parallel", "parallel")),
        name=f"att_g{gi}",
    )(qkv, qkv, qkv, qkv, qkv, qkv, qkv)


def _split3(x):
    hi = x.astype(BF16)
    r1 = x - hi.astype(F32)
    mid = r1.astype(BF16)
    lo = (r1 - mid.astype(F32)).astype(BF16)
    return hi, mid, lo


def _mlstm_kernel(qf_ref, kf_ref, vf_ref, gcf_ref, grf_ref, qb_ref, kb_ref, vb_ref, gcb_ref, grb_ref,
                  of_ref, ob_ref, c_scr, m_scr):
    @pl.when(pl.program_id(1) == 0)
    def _():
        c_scr[...] = jnp.zeros_like(c_scr)
        m_scr[...] = jnp.zeros_like(m_scr)

    _mlstm_direction(0, qf_ref, kf_ref, vf_ref, gcf_ref, grf_ref, of_ref, c_scr, m_scr)
    _mlstm_direction(1, qb_ref, kb_ref, vb_ref, gcb_ref, grb_ref, ob_ref, c_scr, m_scr)


def _mlstm_direction(d, q_ref, k_ref, v_ref, gc_ref, gr_ref, o_ref, c_scr, m_scr):
    L = MLSTM_CHUNK
    q = q_ref[0, 0]
    k = k_ref[0, 0]
    v = v_ref[0, 0]
    gc = gc_ref[0, 0]
    gr = gr_ref[0, 0]

    ti = lax.broadcasted_iota(jnp.int32, (L, L), 0)
    si = lax.broadcasted_iota(jnp.int32, (L, L), 1)
    tri = (si <= ti) if d == 0 else (si >= ti)
    tri_t = (ti <= si) if d == 0 else (ti >= si)
    one = jnp.ones((), BF16)
    tri_b = jnp.where(tri, 1.0, 0.0).astype(BF16)
    tri_tb = jnp.where(tri_t, 1.0, 0.0).astype(BF16)
    b_col = sum(jnp.dot(tri_b, part, preferred_element_type=F32) for part in _split3(_log_sigmoid(gc)))
    b_row = sum(jnp.dot(part, tri_tb, preferred_element_type=F32) for part in _split3(_log_sigmoid(gr)))
    lane = lax.broadcasted_iota(jnp.int32, (1, HEAD_PAD), 1)

    for h in range(HEADS):
        sl = slice(h * HEAD_PAD, (h + 1) * HEAD_PAD)
        st = d * HEADS + h
        li_c, b_c = gc[:, h:h + 1], b_col[:, HEADS + h:HEADS + h + 1]
        li_r, b_r = gr[h:h + 1, :], b_row[HEADS + h:HEADS + h + 1, :]
        m_old = m_scr[st]
        dmat = jnp.where(tri, b_c + (li_r - b_r), -jnp.inf)
        inter = b_c + m_old
        m_row = jnp.maximum(inter, jnp.max(dmat, axis=1, keepdims=True))
        wgt = jnp.exp(dmat - m_row)
        a = jnp.exp(inter - m_row)
        qh = q[:, sl]
        kh = k[:, sl]
        vh = jnp.where(lane == HEAD_DIM, one, v[:, sl])
        sqk = lax.dot_general(qh, kh, (((1,), (1,)), ((), ())), preferred_element_type=F32) * wgt
        c_old = c_scr[st]
        num = (jnp.dot(sqk.astype(BF16), vh, preferred_element_type=F32)
               + a * jnp.dot(qh, c_old.astype(BF16), preferred_element_type=F32))
        den = num[:, HEAD_DIM:HEAD_DIM + 1]
        hj = num / jnp.maximum(jnp.abs(den), jnp.exp(-m_row))
        o_ref[0, :, sl] = jnp.where(lane < HEAD_DIM, hj, 0.0).astype(o_ref.dtype)

        b_last = b_c[L - 1:L] if d == 0 else b_c[0:1]
        g = b_last - b_c + li_c
        m_new = jnp.maximum(b_last + m_old, jnp.max(g, axis=0, keepdims=True))
        kw = (kh.astype(F32) * jnp.exp(g - m_new)).astype(BF16)
        decay = jnp.exp(b_last + m_old - m_new)
        c_scr[st] = decay * c_old + lax.dot_general(kw, vh, (((0,), (0,)), ((), ())),
                                                    preferred_element_type=F32)
        m_scr[st] = m_new


def _mlstm(proj, gates_col, gates_row, batch, seq):
    L = MLSTM_CHUNK
    nc = seq // L
    view = proj.reshape(proj.shape[0], batch, seq, WIDTH_PAD)

    def pos(d, c):
        return c if d == 0 else nc - 1 - c

    def direction_specs(d):
        main = lambda which: pl.BlockSpec((1, 1, L, WIDTH_PAD), lambda b, c: (which, b, pos(d, c), 0))
        return [main(0), main(1), main(2),
                pl.BlockSpec((1, 1, L, 2 * HEADS), lambda b, c: (d, b, pos(d, c), 0)),
                pl.BlockSpec((1, 1, 2 * HEADS, L), lambda b, c: (d, b, 0, pos(d, c)))]

    out_shape = jax.ShapeDtypeStruct((batch, seq, WIDTH_PAD), BF16)
    hf, hb = pl.pallas_call(
        _mlstm_kernel,
        grid=(batch, nc),
        in_specs=direction_specs(0) + direction_specs(1),
        out_specs=[pl.BlockSpec((1, L, WIDTH_PAD), lambda b, c: (b, pos(0, c), 0)),
                   pl.BlockSpec((1, L, WIDTH_PAD), lambda b, c: (b, pos(1, c), 0))],
        out_shape=[out_shape, out_shape],
        scratch_shapes=[pltpu.VMEM((2 * HEADS, HEAD_PAD, HEAD_PAD), F32),
                        pltpu.VMEM((2 * HEADS, 1, 1), F32)],
        compiler_params=_cparams(("parallel", "arbitrary")),
        name="mlstm",
    )(view, view, view, gates_col, gates_row, view, view, view, gates_col, gates_row)
    return hf.reshape(batch * seq, WIDTH_PAD), hb.reshape(batch * seq, WIDTH_PAD)


def _merge_kernel(x_ref, o0_ref, o1_ref, o2_ref, l0_ref, l1_ref, l2_ref, hf_ref, hb_ref, om_ref,
                  xq_ref, g0_ref, g1_ref, g2_ref, xk_ref, xv_ref, hg_ref, wb0_ref, wb1_ref, wb2_ref,
                  wo_ref, out_ref):
    l0, l1, l2 = l0_ref[...], l1_ref[...], l2_ref[...]
    lm = jnp.maximum(jnp.maximum(l0, l1), l2)
    e0, e1, e2 = jnp.exp(l0 - lm), jnp.exp(l1 - lm), jnp.exp(l2 - lm)
    inv = 1.0 / (e0 + e1 + e2)
    p_att = None
    for gi, (o_ref, e) in enumerate(((o0_ref, e0), (o1_ref, e1), (o2_ref, e2))):
        a = (o_ref[...].astype(F32) * (e * inv)).astype(BF16)
        t = jnp.dot(a, wb0_ref[gi * ATT_GROUP_WIDTH:(gi + 1) * ATT_GROUP_WIDTH, :],
                    preferred_element_type=F32)
        p_att = t if p_att is None else p_att + t

    hs = hf_ref[...].astype(F32) + hb_ref[...].astype(F32)
    parts = []
    for h in range(HEADS):
        hh = hs[:, h * HEAD_PAD:(h + 1) * HEAD_PAD]
        ms = jnp.sum(hh * hh, axis=1, keepdims=True) * (1.0 / HEAD_DIM)
        parts.append(hh * lax.rsqrt(ms + EPS))
    hm = (jnp.concatenate(parts, axis=1) * hg_ref[...] * om_ref[0].astype(F32)).astype(BF16)
    p_ml = jnp.dot(hm, wb1_ref[...], preferred_element_type=F32)

    xq = xq_ref[0]
    xk = xk_ref[0]
    xv = xv_ref[0]
    parts = []
    for h in range(HEADS):
        sl = slice(h * HEAD_PAD, (h + 1) * HEAD_PAD)
        s = lax.dot_general(xq[:, sl], xk[:, sl], (((1,), (1,)), ((), ())), preferred_element_type=F32)
        m = jnp.max(s, axis=1, keepdims=True)
        p = jnp.exp(s - m)
        l = jnp.sum(p, axis=1, keepdims=True)
        parts.append(jnp.dot(p.astype(BF16), xv[:, sl], preferred_element_type=F32) * (1.0 / l))
    xo = jnp.concatenate(parts, axis=1).astype(BF16)
    p_x = jnp.dot(xo, wb2_ref[...], preferred_element_type=F32)

    merged = (g0_ref[0].astype(F32) * p_att + g1_ref[0].astype(F32) * p_ml + g2_ref[0].astype(F32) * p_x)
    out_ref[...] = x_ref[...] + jnp.dot(merged.astype(BF16), wo_ref[...], preferred_element_type=F32)


def _merge(x, att, hf, hb, proj, memkv, hg, wb0, wb1, wb2, wo, batch, seq):
    n_tok = batch * seq
    t = MERGE_TILE
    tps = seq // t
    tok = lambda w: pl.BlockSpec((t, w), lambda i: (i, 0))
    chunk = lambda c: pl.BlockSpec((1, t, WIDTH_PAD), lambda i: (c, i, 0))
    const = lambda shape: pl.BlockSpec(shape, lambda i: (0,) * len(shape))
    (o0, l0), (o1, l1), (o2, l2) = att
    return pl.pallas_call(
        _merge_kernel,
        grid=(n_tok // t,),
        in_specs=[tok(D_MODEL),
                  tok(ATT_GROUP_WIDTH), tok(ATT_GROUP_WIDTH), tok(ATT_GROUP_WIDTH),
                  tok(ATT_GROUP_WIDTH), tok(ATT_GROUP_WIDTH), tok(ATT_GROUP_WIDTH),
                  tok(WIDTH_PAD), tok(WIDTH_PAD),
                  chunk(3), chunk(4), chunk(5), chunk(6), chunk(7),
                  pl.BlockSpec((1, N_MEM_ROWS, WIDTH_PAD), lambda i: (0, i // tps, 0)),
                  pl.BlockSpec((1, N_MEM_ROWS, WIDTH_PAD), lambda i: (1, i // tps, 0)),
                  const((1, WIDTH_PAD)), const((ATT_WIDTH, D_MODEL)), const((WIDTH_PAD, D_MODEL)),
                  const((WIDTH_PAD, D_MODEL)), const((D_MODEL, D_MODEL))],
        out_specs=tok(D_MODEL),
        out_shape=jax.ShapeDtypeStruct((n_tok, D_MODEL), F32),
        compiler_params=_cparams(("parallel",)),
        name="merge",
    )(x, o0, o1, o2, l0, l1, l2, hf, hb, proj, proj, proj, proj, proj, memkv, memkv,
      hg, wb0, wb1, wb2, wo)


def _gelu_tanh(x):
    return 0.5 * x * (1.0 + jnp.tanh(math.sqrt(2.0 / math.pi) * (x + 0.044715 * (x * x * x))))


def _ffn_kernel(x_ref, xp_ref, xn_ref, g_ref, wa_ref, wv_ref, cwa_ref, cwv_ref, cba_ref, cbv_ref,
                wd_ref, o_ref, h_scr, acc_scr, *, tiles_per_seq):
    t = x_ref.shape[0]
    i = pl.program_id(0)
    j = pl.program_id(1)

    hr = HALO_ROWS

    @pl.when(j == 0)
    def _():
        g = g_ref[...]
        pos = i % tiles_per_seq
        has_prev = (pos > 0).astype(F32)
        has_next = (pos < tiles_per_seq - 1).astype(F32)
        h_scr[0:hr] = (_rmsnorm_rows(xp_ref[...], g) * has_prev).astype(BF16)
        h_scr[hr:hr + t] = _rmsnorm_rows(x_ref[...], g).astype(BF16)
        h_scr[hr + t:] = (_rmsnorm_rows(xn_ref[...], g) * has_next).astype(BF16)
        acc_scr[...] = jnp.zeros_like(acc_scr)

    h = h_scr[...]

    def conv_branch(w_ref, cw_ref, cb_ref):
        zz = jnp.dot(h, w_ref[...], preferred_element_type=F32)
        n = zz.shape[0]
        cw = cw_ref[...]
        u = cw[0:1] * pltpu.roll(zz, 1, 0) + cw[1:2] * zz + cw[2:3] * pltpu.roll(zz, n - 1, 0) + cb_ref[...]
        return u[hr:hr + t]

    act = _gelu_tanh(conv_branch(wa_ref, cwa_ref, cba_ref)) * conv_branch(wv_ref, cwv_ref, cbv_ref)
    acc_scr[...] += jnp.dot(act.astype(BF16), wd_ref[...], preferred_element_type=F32)

    @pl.when(j == pl.num_programs(1) - 1)
    def _():
        o_ref[...] = x_ref[...] + acc_scr[...]


def _ffn(x, g, w_up, conv_w, conv_b, w_down, seq):
    n_tok, d = x.shape
    t = FFN_TILE
    fc = FFN_CHUNK
    nf = FFN_DIM // fc
    hb = t // HALO_ROWS
    nhalo = n_tok // HALO_ROWS
    return pl.pallas_call(
        functools.partial(_ffn_kernel, tiles_per_seq=seq // t),
        grid=(n_tok // t, nf),
        in_specs=[pl.BlockSpec((t, d), lambda i, j: (i, 0)),
                  pl.BlockSpec((HALO_ROWS, d), lambda i, j: (jnp.maximum(i * hb - 1, 0), 0)),
                  pl.BlockSpec((HALO_ROWS, d), lambda i, j: (jnp.minimum((i + 1) * hb, nhalo - 1), 0)),
                  pl.BlockSpec((1, d), lambda i, j: (0, 0)),
                  pl.BlockSpec((d, fc), lambda i, j: (0, j)),
                  pl.BlockSpec((d, fc), lambda i, j: (0, nf + j)),
                  pl.BlockSpec((3, fc), lambda i, j: (0, j)),
                  pl.BlockSpec((3, fc), lambda i, j: (0, nf + j)),
                  pl.BlockSpec((1, fc), lambda i, j: (0, j)),
                  pl.BlockSpec((1, fc), lambda i, j: (0, nf + j)),
                  pl.BlockSpec((fc, d), lambda i, j: (j, 0))],
        out_specs=pl.BlockSpec((t, d), lambda i, j: (i, 0)),
        out_shape=jax.ShapeDtypeStruct((n_tok, d), F32),
        scratch_shapes=[pltpu.VMEM((t + 2 * HALO_ROWS, d), BF16), pltpu.VMEM((t, d), F32)],
        compiler_params=_cparams(("parallel", "arbitrary")),
        name="ffn",
    )(x, x, x, g.reshape(1, d), w_up, w_up, conv_w, conv_w, conv_b, conv_b, w_down)


def _pad_heads_cols(w):
    lead = w.shape[:-1]
    w = w.reshape(lead + (HEADS, HEAD_DIM))
    w = jnp.pad(w, [(0, 0)] * len(lead) + [(0, 0), (0, HEAD_PAD - HEAD_DIM)])
    return w.reshape(lead + (WIDTH_PAD,))


def _pad_heads_rows(w):
    return _pad_heads_cols(w.T).T


MAIN_MODES = ("conv_silu", "conv_silu", "raw", "sig", "norm192", "sig", "sig", "sig")
MEM_MODES = ("norm192", "raw")


def _prep_layer(p):
    w_in = p["w_in"]
    a = ATT_WIDTH
    segs = {}
    off = 0
    for name, size in (("aq", a), ("ak", a), ("av", a), ("mq", WIDTH), ("mk", WIDTH), ("mv", WIDTH),
                       ("mo", WIDTH), ("mif", N_GATES), ("xq", WIDTH), ("gpre", 3 * D_MODEL)):
        segs[name] = w_in[:, off:off + size]
        off += size
    out = {}
    out["w_att"] = jnp.concatenate([segs["aq"], segs["ak"], segs["av"]], axis=1).astype(BF16)
    q_gain = jnp.tile(p["att_q_g"], ATT_HEADS) * (ATT_HEAD_DIM ** -0.5)
    k_gain = jnp.tile(p["att_k_g"], ATT_HEADS)
    out["att_gains"] = jnp.stack([q_gain, k_gain, jnp.ones_like(k_gain)])[:, None, :]
    out["w_main"] = jnp.concatenate(
        [_pad_heads_cols(segs[n]) for n in ("mq", "mk", "mv", "mo", "xq")] + [segs["gpre"]],
        axis=1).astype(BF16)
    xq_gain = _pad_heads_cols(jnp.tile(p["xatt_q_g"], HEADS)) * (HEAD_DIM ** -0.5)
    ones = jnp.ones((WIDTH_PAD,), F32)
    zeros = jnp.zeros((WIDTH_PAD,), F32)
    n_main = len(MAIN_MODES)
    out["main_gains"] = jnp.stack([ones * (HEAD_DIM ** -0.5)] + [ones] * 3 + [xq_gain] + [ones] * 3)[:, None, :]
    cw, cb = p["mlstm_conv_w"], p["mlstm_conv_b"]
    zeros3 = jnp.zeros((3, WIDTH_PAD), F32)
    out["main_cw"] = jnp.stack([_pad_heads_cols(cw[:, :WIDTH]), _pad_heads_cols(cw[:, WIDTH:])]
                               + [zeros3] * (n_main - 2))
    out["main_cb"] = jnp.stack([_pad_heads_cols(cb[:WIDTH]), _pad_heads_cols(cb[WIDTH:])]
                               + [zeros] * (n_main - 2))[:, None, :]
    out["w_gates"] = jnp.pad(segs["mif"], ((0, 0), (0, GATE_PAD - N_GATES))).astype(BF16)
    out["b_gates"] = jnp.pad(p["mlstm_gate_b"], (0, GATE_PAD - N_GATES))[None, :]
    wkv = p["w_mem_kv"]
    out["w_mem"] = jnp.concatenate([_pad_heads_cols(wkv[:, :WIDTH]), _pad_heads_cols(wkv[:, WIDTH:])],
                                   axis=1).astype(BF16)
    xk_gain = _pad_heads_cols(jnp.tile(p["xatt_k_g"], HEADS))
    out["mem_gains"] = jnp.stack([xk_gain, ones])[:, None, :]
    out["hg"] = _pad_heads_cols(p["mlstm_h_g"])[None, :]
    wb = p["w_branch"]
    out["wb0"] = wb[0].astype(BF16)
    out["wb1"] = _pad_heads_rows(wb[1]).astype(BF16)
    out["wb2"] = _pad_heads_rows(wb[2]).astype(BF16)
    out["wo"] = p["w_out"].astype(BF16)
    out["w_up"] = p["w_up"].astype(BF16)
    out["w_down"] = p["w_down"].astype(BF16)
    out["ffn_cw"] = p["ffn_conv_w"]
    out["ffn_cb"] = p["ffn_conv_b"][None, :]
    out["norm_mix_g"], out["norm_mem_g"], out["norm_ffn_g"] = p["norm_mix_g"], p["norm_mem_g"], p["norm_ffn_g"]
    return out


def _layer(x, mem, w, batch, seq):
    qkv = _attproj(x, w["norm_mix_g"], w["w_att"], w["att_gains"], batch, seq)
    proj, gates = _proj(x, w["norm_mix_g"], w["w_main"], w["main_gains"], MAIN_MODES, PROJ_TILE, seq=seq,
                        conv=(w["main_cw"], w["main_cb"], w["w_gates"], w["b_gates"]))
    memkv = _proj(mem, w["norm_mem_g"], w["w_mem"], w["mem_gains"], MEM_MODES, N_MEM_ROWS)
    att = [_attention_group(qkv[gi], gi, batch, seq) for gi in range(len(ATT_GROUPS))]
    gt = gates[:, :N_GATES].reshape(batch, seq, 2, 2 * HEADS)
    gates_col = gt.transpose(2, 0, 1, 3)
    gates_row = gt.transpose(2, 0, 3, 1)
    hf, hb = _mlstm(proj, gates_col, gates_row, batch, seq)
    x = _merge(x, att, hf, hb, proj, memkv, w["hg"], w["wb0"], w["wb1"], w["wb2"], w["wo"], batch, seq)
    return _ffn(x, w["norm_ffn_g"], w["w_up"], w["ffn_cw"], w["ffn_cb"], w["w_down"], seq)


def _trunk(x, mem, layers):
    batch, seq, d = x.shape
    xf = x.reshape(batch * seq, d)
    memf = mem.reshape(batch * mem.shape[1], d)
    for w in layers:
        xf = _layer(xf, memf, w, batch, seq)
    return xf.reshape(batch, seq, d)


def kernel(x_prompt, x_sample, mem_prompt, mem_sample, norm_mix_g, norm_mem_g, w_in, mlstm_conv_w, mlstm_conv_b, mlstm_gate_b, att_q_g, att_k_g, xatt_q_g, xatt_k_g, w_mem_kv, mlstm_h_g, w_branch, w_out, norm_ffn_g, w_up, ffn_conv_w, ffn_conv_b, w_down):
    params = dict(norm_mix_g=norm_mix_g, norm_mem_g=norm_mem_g, w_in=w_in, mlstm_conv_w=mlstm_conv_w,
                  mlstm_conv_b=mlstm_conv_b, mlstm_gate_b=mlstm_gate_b, att_q_g=att_q_g, att_k_g=att_k_g,
                  xatt_q_g=xatt_q_g, xatt_k_g=xatt_k_g, w_mem_kv=w_mem_kv, mlstm_h_g=mlstm_h_g,
                  w_branch=w_branch, w_out=w_out, norm_ffn_g=norm_ffn_g, w_up=w_up,
                  ffn_conv_w=ffn_conv_w, ffn_conv_b=ffn_conv_b, w_down=w_down)
    layers = [_prep_layer({k: v[l] for k, v in params.items()}) for l in range(DEPTH)]
    return (_trunk(x_prompt, mem_prompt, layers), _trunk(x_sample, mem_sample, layers))
```
